```python
import jax, jax.numpy as jnp
from jax import lax
import numpy as np

D_MODEL = 4096
BATCH = 2
SEQ = 4096
DEPTH = 1
DEC_BATCH = 32
DEC_SEQ = 8
PAST_LEN = 8192
PAGE_SIZE = 128

HEAD_DIM = 128
A_HEADS = 16
A_KV_HEADS = 8
IDX_HEADS = 32
IDX_DIM = 128
IDX_TOPK = 256
B_HEADS = 16
B_KV_HEADS = 8
MOBA_BLOCK = 256
MOBA_TOPK = 3
BRANCH_W = A_HEADS * HEAD_DIM
N_BRANCH = 2
ROPE_THETA = 10000.0
EPS = 1e-6
Q_CHUNK = 64
IN_WIDTHS = (A_HEADS * HEAD_DIM, A_KV_HEADS * HEAD_DIM, A_KV_HEADS * HEAD_DIM,
             IDX_HEADS * IDX_DIM, IDX_DIM, IDX_HEADS, BRANCH_W,
             B_HEADS * HEAD_DIM, B_KV_HEADS * HEAD_DIM, B_KV_HEADS * HEAD_DIM, BRANCH_W,
             N_BRANCH * D_MODEL)
N_IN = sum(IN_WIDTHS)

kernel_name = "hybrid_dsa_moba_gated_decoder_step"

F32 = jnp.float32


def _rms_norm(x, gain):
    xf = x.astype(F32)
    y = xf * lax.rsqrt(jnp.mean(xf * xf, axis=-1, keepdims=True) + EPS)
    return (y * gain.astype(F32)).astype(x.dtype)


def _rope(x, pos):
    half = x.shape[-1] // 2
    inv_freq = ROPE_THETA ** (-jnp.arange(half, dtype=F32) / half)
    ang = pos.astype(F32)[:, None] * inv_freq[None, :]
    cos, sin = jnp.cos(ang)[:, None, :], jnp.sin(ang)[:, None, :]
    xf = x.astype(F32)
    x1, x2 = xf[..., :half], xf[..., half:]
    return jnp.concatenate([x1 * cos - x2 * sin, x2 * cos + x1 * sin], axis=-1).astype(x.dtype)


def _split_cols(proj):
    offsets = np.cumsum(np.array(IN_WIDTHS))[:-1].tolist()
    return jnp.split(proj, offsets, axis=-1)


def _project(x, pos, norm_gain, w_in, q_norm_a, k_norm_a, q_norm_b, k_norm_b):
    b, t, _ = x.shape
    h = _rms_norm(x, norm_gain)
    proj = jnp.einsum("btd,dn->btn", h, w_in)
    a_q, a_k, a_v, i_q, i_k, i_w, a_z, b_q, b_k, b_v, b_z, gate = _split_cols(proj)
    heads = lambda u, n: u.reshape(b, t, n, -1)
    return {
        "a_q": _rope(_rms_norm(heads(a_q, A_HEADS), q_norm_a), pos),
        "a_k": _rope(_rms_norm(heads(a_k, A_KV_HEADS), k_norm_a), pos),
        "a_v": heads(a_v, A_KV_HEADS),
        "idx_q": _rope(heads(i_q, IDX_HEADS), pos),
        "idx_k": _rope(i_k[:, :, None, :], pos)[:, :, 0, :],
        "idx_w": i_w * IDX_HEADS ** -0.5,
        "a_z": a_z,
        "b_q": _rope(_rms_norm(heads(b_q, B_HEADS), q_norm_b), pos),
        "b_k": _rope(_rms_norm(heads(b_k, B_KV_HEADS), k_norm_b), pos),
        "b_v": heads(b_v, B_KV_HEADS),
        "b_z": b_z,
        "gate": jax.nn.sigmoid(gate.reshape(b, t, N_BRANCH, D_MODEL)),
    }


def _merge(x, o_a, o_b, pr, w_out):
    branches = jnp.stack([o_a * jax.nn.silu(pr["a_z"]), o_b * jax.nn.silu(pr["b_z"])], axis=2)
    per_branch = jnp.einsum("btnw,nwd->btnd", branches, w_out)
    return x + jnp.sum(pr["gate"] * per_branch, axis=2)


def _unchunk(out):
    return jnp.moveaxis(out, 0, 1).reshape(out.shape[1], -1, out.shape[-1])


_take_rows = jax.vmap(lambda rows, idx: rows[idx])


def _index_topk(idx_q, idx_w, idx_k, q_pos, n_keep):
    logits = jnp.einsum("bqhe,bse->bqhs", idx_q, idx_k, preferred_element_type=F32)
    score = jnp.einsum("bqhs,bqh->bqs", jax.nn.relu(logits), idx_w.astype(F32)) * IDX_DIM ** -0.5
    k_pos = jnp.arange(idx_k.shape[1])
    score = jnp.where(k_pos[None, None, :] <= q_pos[None, :, None], score, -jnp.inf)
    _, sel = lax.top_k(score, n_keep)
    return sel, sel <= q_pos[None, :, None]


def _attend_gathered(q, k, v, valid):
    b, t, h, d = q.shape
    g = k.shape[-2]
    qg = q.reshape(b, t, g, h // g, d)
    s = jnp.einsum("btgrd,btkgd->btgrk", qg, k, preferred_element_type=F32) * d ** -0.5
    s = jnp.where(valid[:, :, None, None, :], s, -jnp.inf)
    p = jax.nn.softmax(s, axis=-1).astype(v.dtype)
    return jnp.einsum("btgrk,btkgd->btgrd", p, v).reshape(b, t, h * d)


def _dsa_prompt(pr):
    q, k, v = pr["a_q"], pr["a_k"], pr["a_v"]
    s = q.shape[1]
    n_keep = min(IDX_TOPK, s // 4)

    def chunk(c0):
        sl = lambda u: lax.dynamic_slice_in_dim(u, c0, Q_CHUNK, axis=1)
        q_pos = c0 + jnp.arange(Q_CHUNK)
        sel, valid = _index_topk(sl(pr["idx_q"]), sl(pr["idx_w"]), pr["idx_k"], q_pos, n_keep)
        return _attend_gathered(sl(q), _take_rows(k, sel), _take_rows(v, sel), valid)

    return _unchunk(lax.map(chunk, jnp.arange(s // Q_CHUNK) * Q_CHUNK))


def _fetch_rows(pool, layer, new, page_table, pos):
    page = pool.shape[2]
    past_len = page_table.shape[1] * page
    past_pos = jnp.minimum(pos, past_len - 1)
    phys = _take_rows(page_table, past_pos // page)
    from_pool = pool[layer, phys, past_pos % page]
    from_new = _take_rows(new, jnp.clip(pos - past_len, 0, new.shape[1] - 1))
    in_past = (pos < past_len)[..., None, None]
    return jnp.where(in_past, from_pool, from_new)


def _dsa_sample(pr, layer, cache_k, cache_v, cache_idx_k, page_table):
    db, t = pr["a_q"].shape[:2]
    past_len = page_table.shape[1] * cache_k.shape[2]
    n_keep = min(IDX_TOPK, (past_len + t) // 4)
    past_idx_k = cache_idx_k[layer, page_table].reshape(db, past_len, IDX_DIM)
    idx_k = jnp.concatenate([past_idx_k, pr["idx_k"]], axis=1)
    q_pos = past_len + jnp.arange(t)
    sel, valid = _index_topk(pr["idx_q"], pr["idx_w"], idx_k, q_pos, n_keep)
    k = _fetch_rows(cache_k, layer, pr["a_k"], page_table, sel)
    v = _fetch_rows(cache_v, layer, pr["a_v"], page_table, sel)
    return _attend_gathered(pr["a_q"], k, v, valid)


def _block_gate(q, k_mean):
    b, c, h, d = q.shape
    g = k_mean.shape[2]
    s = jnp.einsum("bqgrd,bngd->bqgrn", q.reshape(b, c, g, h // g, d).astype(F32), k_mean)
    return s.reshape(b, c, h, -1)


def _moba_attend(q, k_own, v_own, own_valid, k_sel=None, v_sel=None, sel_valid=None):
    b, c, h, d = q.shape
    g = k_own.shape[2]
    r = h // g
    scale = d ** -0.5
    qg = q.reshape(b, c, g, r, d)
    s_own = jnp.einsum("bqgrd,bsgd->bqgrs", qg, k_own, preferred_element_type=F32) * scale
    m = s_own.shape[-1]
    s_own = jnp.where(own_valid[None, :, None, None, :], s_own, -jnp.inf).reshape(b, c, h, m)
    if k_sel is None:
        p = jax.nn.softmax(s_own, axis=-1).astype(v_own.dtype).reshape(b, c, g, r, m)
        return jnp.einsum("bqgrs,bsgd->bqgrd", p, v_own).reshape(b, c, h * d)
    n, blk = k_sel.shape[3], k_sel.shape[4]
    s_sel = jnp.einsum("bqhd,bqhnsd->bqhns", q, k_sel, preferred_element_type=F32) * scale
    if sel_valid is not None:
        s_sel = jnp.where(sel_valid[..., None], s_sel, -jnp.inf)
    p = jax.nn.softmax(jnp.concatenate([s_sel.reshape(b, c, h, n * blk), s_own], axis=-1), axis=-1)
    p = p.astype(v_own.dtype)
    p_sel = p[..., : n * blk].reshape(b, c, h, n, blk)
    p_own = p[..., n * blk:].reshape(b, c, g, r, m)
    o = jnp.einsum("bqhns,bqhnsd->bqhd", p_sel, v_sel)
    o = o + jnp.einsum("bqgrs,bsgd->bqgrd", p_own, v_own).reshape(b, c, h, d)
    return o.reshape(b, c, h * d)


def _moba_prompt(pr):
    q, k, v = pr["b_q"], pr["b_k"], pr["b_v"]
    b, s, h, d = q.shape
    g = k.shape[2]
    nb = -(-s // MOBA_BLOCK)
    pad = ((0, 0), (0, nb * MOBA_BLOCK - s), (0, 0), (0, 0))
    k_blk = jnp.pad(k, pad).reshape(b, nb, MOBA_BLOCK, g, d)
    v_blk = jnp.pad(v, pad).reshape(b, nb, MOBA_BLOCK, g, d)
    n_sel = min(MOBA_TOPK, (s - 1) // MOBA_BLOCK)
    k_mean = jnp.mean(k_blk.astype(F32), axis=2) if n_sel > 0 else None
    b_ix = jnp.arange(b)[:, None, None, None]
    g_ix = (jnp.arange(h) // (h // g))[None, None, :, None]

    def chunk(c0):
        q_c = lax.dynamic_slice_in_dim(q, c0, Q_CHUNK, axis=1)
        q_pos = c0 + jnp.arange(Q_CHUNK)
        own = c0 // MOBA_BLOCK
        k_own = lax.dynamic_index_in_dim(k_blk, own, axis=1, keepdims=False)
        v_own = lax.dynamic_index_in_dim(v_blk, own, axis=1, keepdims=False)
        own_valid = (own * MOBA_BLOCK + jnp.arange(MOBA_BLOCK))[None, :] <= q_pos[:, None]
        if n_sel == 0:
            return _moba_attend(q_c, k_own, v_own, own_valid)
        gate = jnp.where(jnp.arange(nb) < own, _block_gate(q_c, k_mean), -jnp.inf)
        _, sel = lax.top_k(gate, n_sel)
        k_sel = k_blk[b_ix, sel, :, g_ix]
        v_sel = v_blk[b_ix, sel, :, g_ix]
        return _moba_attend(q_c, k_own, v_own, own_valid, k_sel, v_sel, sel < own)

    return _unchunk(lax.map(chunk, jnp.arange(s // Q_CHUNK) * Q_CHUNK))


def _moba_sample(pr, layer, cache_k, cache_v, page_table):
    q, k_new, v_new = pr["b_q"], pr["b_k"], pr["b_v"]
    db, t, h, d = q.shape
    g = k_new.shape[2]
    page = cache_k.shape[2]
    past_len = page_table.shape[1] * page
    pages_per_block = MOBA_BLOCK // page
    n_full = past_len // MOBA_BLOCK
    own_start = n_full * MOBA_BLOCK
    n_own_past = past_len - own_start
    own_pages = page_table[:, own_start // page:]
    k_own = jnp.concatenate([cache_k[layer, own_pages].reshape(db, n_own_past, g, d), k_new], axis=1)
    v_own = jnp.concatenate([cache_v[layer, own_pages].reshape(db, n_own_past, g, d), v_new], axis=1)
    q_pos = past_len + jnp.arange(t)
    own_valid = (own_start + jnp.arange(n_own_past + t))[None, :] <= q_pos[:, None]
    n_sel = min(MOBA_TOPK, n_full)
    if n_sel == 0:
        return _moba_attend(q, k_own, v_own, own_valid)
    blk_pages = page_table[:, : n_full * pages_per_block].reshape(db, n_full, pages_per_block)
    k_mean = jnp.mean(cache_k[layer, blk_pages].astype(F32), axis=(2, 3))
    _, sel = lax.top_k(_block_gate(q, k_mean), n_sel)
    sel_pages = _take_rows(blk_pages, sel)
    g_ix = (jnp.arange(h) // (h // g))[None, :, None, None]

    def one_seq(args):
        q_b, pages_b, k_own_b, v_own_b = args
        k_sel = cache_k[layer, pages_b, :, g_ix].reshape(t, h, n_sel, MOBA_BLOCK, d)
        v_sel = cache_v[layer, pages_b, :, g_ix].reshape(t, h, n_sel, MOBA_BLOCK, d)
        return _moba_attend(q_b[None], k_own_b[None], v_own_b[None], own_valid, k_sel[None], v_sel[None])[0]

    return lax.map(one_seq, (q, sel_pages, k_own, v_own))


def _prompt_layer(x, norm_gain, w_in, q_norm_a, k_norm_a, q_norm_b, k_norm_b, w_out):
    pos = jnp.arange(x.shape[1])
    pr = _project(x, pos, norm_gain, w_in, q_norm_a, k_norm_a, q_norm_b, k_norm_b)
    y = _merge(x, _dsa_prompt(pr), _moba_prompt(pr), pr, w_out)
    return y, (pr["a_k"], pr["a_v"], pr["idx_k"], pr["b_k"], pr["b_v"])


def _sample_layer(x, layer, cache_a_k, cache_a_v, cache_idx_k, cache_b_k, cache_b_v, page_table,
                  norm_gain, w_in, q_norm_a, k_norm_a, q_norm_b, k_norm_b, w_out):
    past_len = page_table.shape[1] * cache_a_k.shape[2]
    pos = past_len + jnp.arange(x.shape[1])
    pr = _project(x, pos, norm_gain, w_in, q_norm_a, k_norm_a, q_norm_b, k_norm_b)
    o_a = _dsa_sample(pr, layer, cache_a_k, cache_a_v, cache_idx_k, page_table)
    o_b = _moba_sample(pr, layer, cache_b_k, cache_b_v, page_table)
    y = _merge(x, o_a, o_b, pr, w_out)
    return y, (pr["a_k"], pr["a_v"], pr["idx_k"], pr["b_k"], pr["b_v"])


def setup_inputs(seed: int = 0) -> dict:
    key = jax.random.key(seed)
    ks = jax.random.split(key, 16)
    n_pages = PAST_LEN // PAGE_SIZE
    n_pool = (5 * DEC_BATCH * n_pages + 3) // 4
    nrm = lambda k, shape: jax.random.normal(k, shape, F32)
    gain = lambda k, shape: 1.0 + 0.05 * nrm(k, shape)
    kv_a = (DEPTH, n_pool, PAGE_SIZE, A_KV_HEADS, HEAD_DIM)
    kv_b = (DEPTH, n_pool, PAGE_SIZE, B_KV_HEADS, HEAD_DIM)
    perm = jax.random.permutation(ks[7], n_pool)
    page_table = perm[: DEC_BATCH * n_pages].reshape(DEC_BATCH, n_pages).astype(jnp.int32)
    return {
        "x_prompt": nrm(ks[0], (BATCH, SEQ, D_MODEL)),
        "x_sample": nrm(ks[1], (DEC_BATCH, DEC_SEQ, D_MODEL)),
        "cache_a_k": nrm(ks[2], kv_a),
        "cache_a_v": nrm(ks[3], kv_a),
        "cache_idx_k": nrm(ks[4], (DEPTH, n_pool, PAGE_SIZE, IDX_DIM)),
        "cache_b_k": nrm(ks[5], kv_b),
        "cache_b_v": nrm(ks[6], kv_b),
        "page_table": page_table,
        "norm_gain": gain(ks[8], (DEPTH, D_MODEL)),
        "w_in": nrm(ks[9], (DEPTH, D_MODEL, N_IN)) * D_MODEL ** -0.5,
        "q_norm_a": gain(ks[10], (DEPTH, HEAD_DIM)),
        "k_norm_a": gain(ks[11], (DEPTH, HEAD_DIM)),
        "q_norm_b": gain(ks[12], (DEPTH, HEAD_DIM)),
        "k_norm_b": gain(ks[13], (DEPTH, HEAD_DIM)),
        "w_out": nrm(ks[14], (DEPTH, N_BRANCH, BRANCH_W, D_MODEL)) * BRANCH_W ** -0.5,
    }


def reference(x_prompt, x_sample, cache_a_k, cache_a_v, cache_idx_k, cache_b_k, cache_b_v, page_table,
              norm_gain, w_in, q_norm_a, k_norm_a, q_norm_b, k_norm_b, w_out):
    y_prompt, y_sample = x_prompt, x_sample
    rows_p, rows_s = [], []
    for layer in range(DEPTH):
        params = (norm_gain[layer], w_in[layer], q_norm_a[layer], k_norm_a[layer],
                  q_norm_b[layer], k_norm_b[layer], w_out[layer])
        y_prompt, new_p = _prompt_layer(y_prompt, *params)
        y_sample, new_s = _sample_layer(y_sample, layer, cache_a_k, cache_a_v, cache_idx_k,
                                        cache_b_k, cache_b_v, page_table, *params)
        rows_p.append(new_p)
        rows_s.append(new_s)
    stack = lambda rows, i: jnp.stack([r[i] for r in rows], axis=0)
    return (y_prompt, y_sample,
            stack(rows_p, 0), stack(rows_p, 1), stack(rows_p, 2), stack(rows_p, 3), stack(rows_p, 4),
            stack(rows_s, 0), stack(rows_s, 1), stack(rows_s, 2), stack(rows_s, 3), stack(rows_s, 4))
```

```python
import functools

import jax
import jax.numpy as jnp
from jax import lax
from jax.experimental import pallas as pl
from jax.experimental.pallas import tpu as pltpu

F32 = jnp.float32
BF16 = jnp.bfloat16
I32 = jnp.int32

D_MODEL = 4096
HEAD_DIM = 128
A_HEADS = 16
A_KV = 8
IDX_HEADS = 32
IDX_DIM = 128
IDX_TOPK = 256
B_HEADS = 16
B_KV = 8
MOBA_BLOCK = 256
MOBA_TOPK = 3
BRANCH_W = 2048
ROPE_THETA = 10000.0
EPS = 1e-6
PAGE = 128

OFF_AQ, OFF_AK, OFF_AV, OFF_IQ, OFF_AZ = 0, 2048, 3072, 4096, 8192
OFF_BQ, OFF_BK, OFF_BV, OFF_BZ, OFF_GATE = 10240, 12288, 13312, 14336, 16384
N_MAIN = 24576
N_SMALL = 256

NEG = -1e30
INT_MIN = -(2 ** 31)
TQ = 256
VMEM_LIMIT = 56 * 1024 * 1024

_NT = (((1,), (1,)), ((), ()))


def _cp(sem, vmem=None):
    return pltpu.CompilerParams(dimension_semantics=sem, vmem_limit_bytes=vmem)


def _rmsnorm_kernel(x_ref, g_ref, o_ref):
    x = x_ref[...]
    ms = jnp.mean(x * x, axis=-1, keepdims=True)
    o_ref[...] = (x * lax.rsqrt(ms + EPS) * g_ref[...]).astype(BF16)


def _rmsnorm(x, gain, tm):
    m = x.shape[0]
    return pl.pallas_call(
        _rmsnorm_kernel,
        out_shape=jax.ShapeDtypeStruct((m, D_MODEL), BF16),
        grid=(m // tm,),
        in_specs=[pl.BlockSpec((tm, D_MODEL), lambda i: (i, 0)),
                  pl.BlockSpec((1, D_MODEL), lambda i: (0, 0))],
        out_specs=pl.BlockSpec((tm, D_MODEL), lambda i: (i, 0)),
        compiler_params=_cp(("parallel",)),
        name="rmsnorm",
    )(x, gain)


def _matmul_kernel(x_ref, w_ref, o_ref):
    o_ref[...] = jnp.dot(x_ref[...], w_ref[...], preferred_element_type=F32)


def _matmul(x, w, tm, tn, name):
    m, k = x.shape
    n = w.shape[1]
    return pl.pallas_call(
        _matmul_kernel,
        out_shape=jax.ShapeDtypeStruct((m, n), F32),
        grid=(m // tm, n // tn),
        in_specs=[pl.BlockSpec((tm, k), lambda i, j: (i, 0)),
                  pl.BlockSpec((k, tn), lambda i, j: (0, j))],
        out_specs=pl.BlockSpec((tm, tn), lambda i, j: (i, j)),
        compiler_params=_cp(("parallel", "parallel"), VMEM_LIMIT),
        name=name,
    )(x, w)


def _rope(y, cos, sin):
    return y * cos + pltpu.roll(y, 64, 1) * sin


def _head_norm(x, gain):
    ms = jnp.mean(x * x, axis=-1, keepdims=True)
    return x * lax.rsqrt(ms + EPS) * gain


def _post_kernel(pa_ref, piq_ref, pbq_ref, pbkv_ref, ps_ref, cos_ref, sin_ref,
                 qna_ref, kna_ref, qnb_ref, knb_ref,
                 qa_ref, ka32_ref, ka_ref, va32_ref, va_ref, iq_ref, ik32_ref, ik_ref, iw_ref,
                 qb_ref, kb32_ref, kb_ref, vb32_ref, vb_ref, kmean_ref):
    cos = cos_ref[...]
    sin = sin_ref[...]
    scale = HEAD_DIM ** -0.5
    hs = lambda h: slice(h * HEAD_DIM, (h + 1) * HEAD_DIM)

    qna, kna, qnb, knb = qna_ref[...], kna_ref[...], qnb_ref[...], knb_ref[...]
    for h in range(A_HEADS):
        q = _rope(_head_norm(pa_ref[:, hs(h)], qna), cos, sin)
        qa_ref[:, hs(h)] = (q * scale).astype(BF16)
    for h in range(A_KV):
        k = _rope(_head_norm(pa_ref[:, hs(A_HEADS + h)], kna), cos, sin)
        ka32_ref[:, hs(h)] = k
        ka_ref[:, hs(h)] = k.astype(BF16)
        v = pa_ref[:, hs(A_HEADS + A_KV + h)]
        va32_ref[:, hs(h)] = v
        va_ref[:, hs(h)] = v.astype(BF16)
    for h in range(IDX_HEADS):
        iq_ref[:, hs(h)] = _rope(piq_ref[:, hs(h)], cos, sin).astype(BF16)
    ik = _rope(ps_ref[:, 0:IDX_DIM], cos, sin)
    ik32_ref[...] = ik
    ik_ref[...] = ik.astype(BF16)
    iw_ref[...] = ps_ref[:, IDX_DIM:2 * IDX_DIM] * (IDX_HEADS ** -0.5 * IDX_DIM ** -0.5)
    for h in range(B_HEADS):
        q = _rope(_head_norm(pbq_ref[:, hs(h)], qnb), cos, sin)
        qb_ref[:, hs(h)] = (q * scale).astype(BF16)
    for h in range(B_KV):
        k = _rope(_head_norm(pbkv_ref[:, hs(h)], knb), cos, sin)
        kb32_ref[:, hs(h)] = k
        kb_ref[:, hs(h)] = k.astype(BF16)
        kmean_ref[0, :, hs(h)] = jnp.mean(k, axis=0, keepdims=True)
        v = pbkv_ref[:, hs(B_KV + h)]
        vb32_ref[:, hs(h)] = v
        vb_ref[:, hs(h)] = v.astype(BF16)


def _post(proj, small, cos, sin, qna, kna, qnb, knb):
    m = proj.shape[0]
    r = TQ
    nt = cos.shape[0] // r
    row = lambda c: (lambda i: (i, c))
    gain = pl.BlockSpec((1, HEAD_DIM), lambda i: (0, 0))
    tab = pl.BlockSpec((r, HEAD_DIM), lambda i: (i % nt, 0))
    sds = jax.ShapeDtypeStruct
    kv32 = sds((m, 1024), F32)
    kvbf = sds((m, 1024), BF16)
    kvspec = pl.BlockSpec((r, 1024), row(0))
    out_shape = (sds((m, 2048), BF16), kv32, kvbf, kv32, kvbf,
                 sds((m, 4096), BF16), sds((m, 128), F32), sds((m, 128), BF16), sds((m, 128), F32),
                 sds((m, 2048), BF16), kv32, kvbf, kv32, kvbf,
                 sds((m // r, 1, 1024), F32))
    s128 = pl.BlockSpec((r, 128), row(0))
    out_specs = (pl.BlockSpec((r, 2048), row(0)), kvspec, kvspec, kvspec, kvspec,
                 pl.BlockSpec((r, 4096), row(0)), s128, s128, s128,
                 pl.BlockSpec((r, 2048), row(0)), kvspec, kvspec, kvspec, kvspec,
                 pl.BlockSpec((1, 1, 1024), lambda i: (i, 0, 0)))
    return pl.pallas_call(
        _post_kernel,
        out_shape=out_shape,
        grid=(m // r,),
        in_specs=[pl.BlockSpec((r, 4096), row(0)),
                  pl.BlockSpec((r, 4096), row(1)),
                  pl.BlockSpec((r, 2048), row(OFF_BQ // 2048)),
                  pl.BlockSpec((r, 2048), row(OFF_BK // 2048)),
                  pl.BlockSpec((r, N_SMALL), row(0)),
                  tab, tab, gain, gain, gain, gain],
        out_specs=out_specs,
        compiler_params=_cp(("parallel",), VMEM_LIMIT),
        name="post",
    )(proj, proj, proj, proj, small, cos, sin, qna, kna, qnb, knb)


def _sort_key(x):
    bits = lax.bitcast_convert_type(x, I32)
    return jnp.where(bits >= 0, bits, bits ^ jnp.int32(0x7FFFFFFF))


def _silu(z):
    return z * jax.nn.sigmoid(z)


def _softmax_step(q, k, v, bias, m, l, acc):
    s = lax.dot_general(q, k, _NT, preferred_element_type=F32) + bias
    m_new = jnp.maximum(m, jnp.max(s, axis=-1, keepdims=True))
    alpha = jnp.exp(m - m_new)
    p = jnp.exp(s - m_new)
    l = alpha * l + jnp.sum(p, axis=-1, keepdims=True)
    acc = alpha * acc + jnp.dot(p.astype(BF16), v, preferred_element_type=F32)
    return m_new, l, acc


def _dsa_prompt_kernel(iq_ref, iw_ref, ik_ref, qa_ref, ka_ref, va_ref, az_ref, o_ref,
                       wb_ref, key_ref, bias_ref):
    i = pl.program_id(1)
    g = pl.program_id(2)
    n_tiles = i + 1
    half = TQ // 2
    lane = lax.broadcasted_iota(I32, (TQ, 128), 1)
    rowg = lax.broadcasted_iota(I32, (TQ, 128), 0) + i * TQ

    @pl.when(g == 0)
    def _select():
        for h in range(IDX_HEADS):
            wb_ref[h] = jnp.broadcast_to(iw_ref[:, h:h + 1], (TQ, 128))

        def score_tile(j, _):
            ks = ik_ref[pl.ds(pl.multiple_of(j * TQ, TQ), TQ), :]
            for rh in range(2):
                rs = slice(rh * half, (rh + 1) * half)
                acc0 = jnp.zeros((half, 128), F32)
                acc1 = jnp.zeros((half, 128), F32)
                for h in range(IDX_HEADS):
                    lg = lax.dot_general(iq_ref[rs, h * 128:(h + 1) * 128], ks, _NT,
                                         preferred_element_type=F32)
                    lg = jnp.maximum(lg, 0.0)
                    w = wb_ref[h, rs, :]
                    acc0 = acc0 + lg[:, :128] * w
                    acc1 = acc1 + lg[:, 128:] * w
                colg = lax.broadcasted_iota(I32, (half, 128), 1) + j * TQ
                rg = lax.broadcasted_iota(I32, (half, 128), 0) + (i * TQ + rh * half)
                key_ref[j, rs, 0:128] = _sort_key(jnp.where(colg <= rg, acc0, -jnp.inf))
                key_ref[j, rs, 128:256] = _sort_key(jnp.where(colg + 128 <= rg, acc1, -jnp.inf))
            return 0

        lax.fori_loop(0, n_tiles, score_tile, 0)

        def count(pred):
            def body(j, cnt):
                t = key_ref[j]
                colg = lane + j * TQ
                return (cnt + jnp.where(pred(t[:, :128], colg), 1.0, 0.0)
                        + jnp.where(pred(t[:, 128:], colg + 128), 1.0, 0.0))
            cnt = lax.fori_loop(0, n_tiles, body, jnp.zeros((TQ, 128), F32))
            return jnp.sum(cnt, axis=-1, keepdims=True)

        def bit_body(b, thr):
            cand = thr + lax.shift_left(jnp.int32(1), 31 - b)
            n = count(lambda t, c: t >= cand)
            return jnp.where(n >= IDX_TOPK, cand, thr)

        thr = lax.fori_loop(0, 32, bit_body, jnp.full((TQ, 128), INT_MIN, I32))

        need = IDX_TOPK - count(lambda t, c: t > thr)

        def cut_body(b, cut):
            cand = cut + lax.shift_left(jnp.int32(1), 11 - b)
            n = count(lambda t, c: jnp.where(t == thr, c, 4096) < cand)
            return jnp.where(n < need, cand, cut)

        cut = lax.fori_loop(0, 12, cut_body, jnp.zeros((TQ, 128), I32))

        def bias_tile(j, _):
            t = key_ref[j]
            for hh in range(2):
                th = t[:, hh * 128:(hh + 1) * 128]
                colg = lane + (j * TQ + hh * 128)
                tie = jnp.where(th == thr, jnp.where(colg <= cut, 0.0, NEG), NEG)
                picked = jnp.where(th > thr, 0.0, tie)
                bias_ref[j, :, hh * 128:(hh + 1) * 128] = jnp.where(colg <= rowg, picked, NEG)
            return 0

        lax.fori_loop(0, n_tiles, bias_tile, 0)

    for r in range(2):
        cs = slice(r * 128, (r + 1) * 128)
        q = qa_ref[:, cs]

        def body(j, carry):
            off = pl.multiple_of(j * TQ, TQ)
            return _softmax_step(q, ka_ref[pl.ds(off, TQ), :], va_ref[pl.ds(off, TQ), :],
                                 bias_ref[j], *carry)

        m, l, acc = lax.fori_loop(
            0, n_tiles, body,
            (jnp.full((TQ, 1), NEG, F32), jnp.zeros((TQ, 1), F32), jnp.zeros((TQ, 128), F32)))
        o_ref[:, cs] = (acc / l * _silu(az_ref[:, cs])).astype(BF16)


def _dsa_prompt(iq, iw, ik, qa, ka, va, proj, nb, seq):
    nq = seq // TQ
    m = nb * seq
    rowt = lambda b, i, g: b * nq + i
    return pl.pallas_call(
        _dsa_prompt_kernel,
        out_shape=jax.ShapeDtypeStruct((m, BRANCH_W), BF16),
        grid=(nb, nq, A_KV),
        in_specs=[pl.BlockSpec((TQ, 4096), lambda b, i, g: (rowt(b, i, g), 0)),
                  pl.BlockSpec((TQ, 128), lambda b, i, g: (rowt(b, i, g), 0)),
                  pl.BlockSpec((seq, 128), lambda b, i, g: (b, 0)),
                  pl.BlockSpec((TQ, 256), lambda b, i, g: (rowt(b, i, g), g)),
                  pl.BlockSpec((seq, 128), lambda b, i, g: (b, g)),
                  pl.BlockSpec((seq, 128), lambda b, i, g: (b, g)),
                  pl.BlockSpec((TQ, 256), lambda b, i, g: (rowt(b, i, g), OFF_AZ // 256 + g))],
        out_specs=pl.BlockSpec((TQ, 256), lambda b, i, g: (rowt(b, i, g), g)),
        scratch_shapes=[pltpu.VMEM((IDX_HEADS, TQ, 128), F32),
                        pltpu.VMEM((seq // TQ, TQ, TQ), I32),
                        pltpu.VMEM((seq // TQ, TQ, TQ), F32)],
        compiler_params=_cp(("arbitrary", "arbitrary", "arbitrary"), VMEM_LIMIT),
        name="dsa_prompt",
    )(iq, iw, ik, qa, ka, va, proj)


def _top_blocks(gate, n_valid):
    lane = lax.broadcasted_iota(I32, gate.shape, 1).astype(F32)
    valid = lane < n_valid
    rem = jnp.where(valid, gate, -jnp.inf)
    sel = jnp.zeros(gate.shape, F32)
    for _ in range(MOBA_TOPK):
        mx = jnp.max(rem, axis=-1, keepdims=True)
        first = jnp.min(jnp.where(rem == mx, lane, 128.0), axis=-1, keepdims=True)
        pick = lane == first
        sel = jnp.where(pick, 1.0, sel)
        rem = jnp.where(pick, -jnp.inf, rem)
    return jnp.where(valid, sel, 0.0)


def _moba_prompt_kernel(qb_ref, kb_ref, vb_ref, km_ref, bz_ref, o_ref):
    i = pl.program_id(1)
    lane = lax.broadcasted_iota(I32, (TQ, 128), 1)
    row = lax.broadcasted_iota(I32, (TQ, TQ), 0)
    col = lax.broadcasted_iota(I32, (TQ, TQ), 1)
    own_bias = jnp.where(col <= row, 0.0, NEG)
    km = km_ref[...].astype(BF16)

    for r in range(2):
        cs = slice(r * 128, (r + 1) * 128)
        q = qb_ref[:, cs]
        gate = lax.dot_general(q, km, _NT, preferred_element_type=F32)
        sel = _top_blocks(gate, i)

        def body(j, carry):
            off = pl.multiple_of(j * TQ, TQ)
            picked = jnp.max(jnp.where(lane == j, sel, 0.0), axis=-1, keepdims=True)
            bias = jnp.where(picked > 0.0, 0.0, NEG)
            return _softmax_step(q, kb_ref[pl.ds(off, TQ), :], vb_ref[pl.ds(off, TQ), :], bias, *carry)

        carry = lax.fori_loop(
            0, i, body,
            (jnp.full((TQ, 1), NEG, F32), jnp.zeros((TQ, 1), F32), jnp.zeros((TQ, 128), F32)))
        off = pl.multiple_of(i * TQ, TQ)
        m, l, acc = _softmax_step(q, kb_ref[pl.ds(off, TQ), :], vb_ref[pl.ds(off, TQ), :],
                                  own_bias, *carry)
        o_ref[:, cs] = (acc / l * _silu(bz_ref[:, cs])).astype(BF16)


def _moba_prompt(qb, kb, vb, kmean, proj, nb, seq):
    nq = seq // TQ
    m = nb * seq
    rowt = lambda b, i, g: b * nq + i
    return pl.pallas_call(
        _moba_prompt_kernel,
        out_shape=jax.ShapeDtypeStruct((m, BRANCH_W), BF16),
        grid=(nb, nq, B_KV),
        in_specs=[pl.BlockSpec((TQ, 256), lambda b, i, g: (rowt(b, i, g), g)),
                  pl.BlockSpec((seq, 128), lambda b, i, g: (b, g)),
                  pl.BlockSpec((seq, 128), lambda b, i, g: (b, g)),
                  pl.BlockSpec((None, 128, 128), lambda b, i, g: (b, 0, g)),
                  pl.BlockSpec((TQ, 256), lambda b, i, g: (rowt(b, i, g), OFF_BZ // 256 + g))],
        out_specs=pl.BlockSpec((TQ, 256), lambda b, i, g: (rowt(b, i, g), g)),
        compiler_params=_cp(("parallel", "parallel", "parallel"), VMEM_LIMIT),
        name="moba_prompt",
    )(qb, kb, vb, kmean, proj)


def _merge_kernel(x_ref, a_ref, b_ref, wa_ref, wb_ref, ga_ref, gb_ref, o_ref):
    pa = jnp.dot(a_ref[...], wa_ref[...], preferred_element_type=F32)
    pb = jnp.dot(b_ref[...], wb_ref[...], preferred_element_type=F32)
    o_ref[...] = x_ref[...] + (jax.nn.sigmoid(ga_ref[...]) * pa + jax.nn.sigmoid(gb_ref[...]) * pb)


def _merge(x, bra, brb, w_out, proj, tm, tn):
    m = x.shape[0]
    ga0 = OFF_GATE // tn
    gb0 = (OFF_GATE + D_MODEL) // tn
    return pl.pallas_call(
        _merge_kernel,
        out_shape=jax.ShapeDtypeStruct((m, D_MODEL), F32),
        grid=(m // tm, D_MODEL // tn),
        in_specs=[pl.BlockSpec((tm, tn), lambda i, j: (i, j)),
                  pl.BlockSpec((tm, BRANCH_W), lambda i, j: (i, 0)),
                  pl.BlockSpec((tm, BRANCH_W), lambda i, j: (i, 0)),
                  pl.BlockSpec((None, BRANCH_W, tn), lambda i, j: (0, 0, j)),
                  pl.BlockSpec((None, BRANCH_W, tn), lambda i, j: (1, 0, j)),
                  pl.BlockSpec((tm, tn), lambda i, j: (i, ga0 + j)),
                  pl.BlockSpec((tm, tn), lambda i, j: (i, gb0 + j))],
        out_specs=pl.BlockSpec((tm, tn), lambda i, j: (i, j)),
        compiler_params=_cp(("parallel", "parallel"), VMEM_LIMIT),
        name="merge",
    )(x, bra, brb, w_out, w_out, proj, proj)


def _idx_sample_kernel(pt_ref, kp_ref, knew_ref, iq_ref, iw_ref, o_ref, *, n_pages, n_new):
    p = pl.program_id(1)

    def scores(keys, mask_new):
        lg = lax.dot_general(iq_ref[...], keys, _NT, preferred_element_type=F32)
        acc = jnp.zeros((n_new, 128), F32)
        for h in range(IDX_HEADS):
            w = jnp.broadcast_to(iw_ref[:, h:h + 1], (n_new, 128))
            acc = acc + jnp.maximum(lg[h * n_new:(h + 1) * n_new, :], 0.0) * w
        if mask_new:
            t = lax.broadcasted_iota(I32, (n_new, 128), 0)
            c = lax.broadcasted_iota(I32, (n_new, 128), 1)
            acc = jnp.where(c <= t, acc, -jnp.inf)
        o_ref[...] = acc

    @pl.when(p < n_pages)
    def _():
        scores(kp_ref[...].astype(BF16), False)

    @pl.when(p == n_pages)
    def _():
        scores(knew_ref[...], True)


def _idx_sample(pt_flat, cache_idx, knew_pad, iq_ht, iw, nseq, n_pages, n_new):
    kern = functools.partial(_idx_sample_kernel, n_pages=n_pages, n_new=n_new)
    page = lambda b, p, pt: (pt[b * n_pages + jnp.minimum(p, n_pages - 1)], 0, 0)
    grid_spec = pltpu.PrefetchScalarGridSpec(
        num_scalar_prefetch=1,
        grid=(nseq, n_pages + 1),
        in_specs=[pl.BlockSpec((None, PAGE, IDX_DIM), page),
                  pl.BlockSpec((None, PAGE, IDX_DIM), lambda b, p, pt: (b, 0, 0)),
                  pl.BlockSpec((None, IDX_HEADS * n_new, IDX_DIM), lambda b, p, pt: (b, 0, 0)),
                  pl.BlockSpec((n_new, 128), lambda b, p, pt: (b, 0))],
        out_specs=pl.BlockSpec((n_new, 128), lambda b, p, pt: (b, p)),
    )
    return pl.pallas_call(
        kern,
        out_shape=jax.ShapeDtypeStruct((nseq * n_new, (n_pages + 1) * PAGE), F32),
        grid_spec=grid_spec,
        compiler_params=_cp(("arbitrary", "arbitrary")),
        name="idx_sample",
    )(pt_flat, cache_idx, knew_pad, iq_ht, iw)


def _thresh_kernel(s_ref, o_ref, *, n_past, n_new):
    rows, width = s_ref.shape
    key = _sort_key(s_ref[...])
    col = lax.broadcasted_iota(I32, (rows, width), 1)
    t_new = lax.broadcasted_iota(I32, (rows, width), 0) % n_new

    def count(mask):
        return jnp.sum(jnp.where(mask, 1.0, 0.0), axis=-1, keepdims=True)

    def bit_body(b, thr):
        cand = thr + lax.shift_left(jnp.int32(1), 31 - b)
        return jnp.where(count(key >= cand) >= IDX_TOPK, cand, thr)

    thr = lax.fori_loop(0, 32, bit_body, jnp.full((rows, 1), INT_MIN, I32))
    need = IDX_TOPK - count(key > thr)
    tie_col = jnp.where(key == thr, col, 2 ** 14)

    def cut_body(b, cut):
        cand = cut + lax.shift_left(jnp.int32(1), 13 - b)
        return jnp.where(count(tie_col < cand) < need, cand, cut)

    cut = lax.fori_loop(0, 14, cut_body, jnp.zeros((rows, 1), I32))
    tie = jnp.where(key == thr, jnp.where(col <= cut, 0.0, NEG), NEG)
    picked = jnp.where(key > thr, 0.0, tie)
    o_ref[...] = jnp.where(col <= n_past + t_new, picked, NEG)


def _thresh(scores, n_past, n_new, rows):
    m, width = scores.shape
    return pl.pallas_call(
        functools.partial(_thresh_kernel, n_past=n_past, n_new=n_new),
        out_shape=jax.ShapeDtypeStruct((m, width), F32),
        grid=(m // rows,),
        in_specs=[pl.BlockSpec((rows, width), lambda i: (i, 0))],
        out_specs=pl.BlockSpec((rows, width), lambda i: (i, 0)),
        compiler_params=_cp(("parallel",), VMEM_LIMIT),
        name="thresh_sample",
    )(scores)


def _paged_step(qbd, k, v, bias, m_ref, l_ref, acc_ref):
    s = lax.dot_general(qbd, k, _NT, preferred_element_type=F32) + bias
    m_prev = m_ref[...]
    m_new = jnp.maximum(m_prev, jnp.max(s, axis=-1, keepdims=True))
    alpha = jnp.exp(m_prev - m_new)
    p = jnp.exp(s - m_new)
    l_ref[...] = alpha * l_ref[...] + jnp.sum(p, axis=-1, keepdims=True)
    m_ref[...] = m_new
    pv = jnp.dot(p.astype(BF16), v, preferred_element_type=F32)
    acc_ref[...] = acc_ref[...] * jnp.concatenate([alpha] * (acc_ref.shape[1] // 128), axis=1) + pv


def _paged_init(m_ref, l_ref, acc_ref):
    m_ref[...] = jnp.full(m_ref.shape, NEG, F32)
    l_ref[...] = jnp.zeros(l_ref.shape, F32)
    acc_ref[...] = jnp.zeros(acc_ref.shape, F32)


def _paged_finish(z_ref, o_ref, l_ref, acc_ref, n_groups):
    rows = 128 // n_groups
    for g in range(n_groups):
        rs = slice(g * rows, (g + 1) * rows)
        o = acc_ref[rs, g * 128:(g + 1) * 128] / l_ref[rs, :]
        o_ref[rs, :] = (o * _silu(z_ref[rs, :])).astype(BF16)


def _dsa_sample_kernel(pt_ref, kp_ref, vp_ref, knew_ref, vnew_ref, q_ref, bias_ref, z_ref, o_ref,
                       m_ref, l_ref, acc_ref, *, n_pages):
    p = pl.program_id(1)

    @pl.when(p == 0)
    def _():
        _paged_init(m_ref, l_ref, acc_ref)

    bias = jnp.concatenate([bias_ref[...]] * (128 // bias_ref.shape[0]), axis=0)

    @pl.when(p < n_pages)
    def _():
        _paged_step(q_ref[...], kp_ref[...].astype(BF16), vp_ref[...].astype(BF16), bias,
                    m_ref, l_ref, acc_ref)

    @pl.when(p == n_pages)
    def _():
        _paged_step(q_ref[...], knew_ref[...], vnew_ref[...], bias, m_ref, l_ref, acc_ref)
        _paged_finish(z_ref, o_ref, l_ref, acc_ref, A_KV)


def _dsa_sample(pt_flat, cache_k, cache_v, knew, vnew, qbd, bias, z, nseq, n_pages, n_new):
    page = lambda b, p, pt: (pt[b * n_pages + jnp.minimum(p, n_pages - 1)], 0, 0)
    seq3 = lambda b, p, pt: (b, 0, 0)
    grid_spec = pltpu.PrefetchScalarGridSpec(
        num_scalar_prefetch=1,
        grid=(nseq, n_pages + 1),
        in_specs=[pl.BlockSpec((None, PAGE, 1024), page),
                  pl.BlockSpec((None, PAGE, 1024), page),
                  pl.BlockSpec((None, PAGE, 1024), seq3),
                  pl.BlockSpec((None, PAGE, 1024), seq3),
                  pl.BlockSpec((None, 128, 1024), seq3),
                  pl.BlockSpec((n_new, 128), lambda b, p, pt: (b, p)),
                  pl.BlockSpec((None, 128, 128), seq3)],
        out_specs=pl.BlockSpec((None, 128, 128), seq3),
        scratch_shapes=[pltpu.VMEM((128, 128), F32), pltpu.VMEM((128, 128), F32),
                        pltpu.VMEM((128, 1024), F32)],
    )
    return pl.pallas_call(
        functools.partial(_dsa_sample_kernel, n_pages=n_pages),
        out_shape=jax.ShapeDtypeStruct((nseq, 128, 128), BF16),
        grid_spec=grid_spec,
        compiler_params=_cp(("arbitrary", "arbitrary")),
        name="dsa_sample",
    )(pt_flat, cache_k, cache_v, knew, vnew, qbd, bias, z)


def _kmean_kernel(pt_ref, kp_ref, o_ref, *, pages_per_block):
    p = pl.program_id(1)
    s = jnp.sum(kp_ref[...], axis=0, keepdims=True) * (1.0 / MOBA_BLOCK)

    @pl.when(p % pages_per_block == 0)
    def _():
        o_ref[...] = s

    @pl.when(p % pages_per_block != 0)
    def _():
        o_ref[...] = o_ref[...] + s


def _kmean_sample(pt_flat, cache_k, nseq, n_pages):
    ppb = MOBA_BLOCK // PAGE
    grid_spec = pltpu.PrefetchScalarGridSpec(
        num_scalar_prefetch=1,
        grid=(nseq, n_pages),
        in_specs=[pl.BlockSpec((None, PAGE, 1024), lambda b, p, pt: (pt[b * n_pages + p], 0, 0))],
        out_specs=pl.BlockSpec((None, None, 1, 1024), lambda b, p, pt: (b, p // ppb, 0, 0)),
    )
    return pl.pallas_call(
        functools.partial(_kmean_kernel, pages_per_block=ppb),
        out_shape=jax.ShapeDtypeStruct((nseq, n_pages // ppb, 1, 1024), F32),
        grid_spec=grid_spec,
        compiler_params=_cp(("arbitrary", "arbitrary")),
        name="kmean_sample",
    )(pt_flat, cache_k)


def _moba_sample_kernel(pt_ref, kp_ref, vp_ref, knew_ref, vnew_ref, q_ref, km_ref, z_ref, o_ref,
                        m_ref, l_ref, acc_ref, sel_ref, *, n_pages, n_blocks, n_new):
    p = pl.program_id(1)
    lane = lax.broadcasted_iota(I32, (128, 128), 1)

    @pl.when(p == 0)
    def _():
        _paged_init(m_ref, l_ref, acc_ref)
        gate = lax.dot_general(q_ref[...], km_ref[...].astype(BF16), _NT, preferred_element_type=F32)
        sel_ref[...] = _top_blocks(gate, n_blocks)

    @pl.when(p < n_pages)
    def _():
        blk = p // (MOBA_BLOCK // PAGE)
        picked = jnp.max(jnp.where(lane == blk, sel_ref[...], 0.0), axis=-1, keepdims=True)
        bias = jnp.where(picked > 0.0, 0.0, NEG)
        _paged_step(q_ref[...], kp_ref[...].astype(BF16), vp_ref[...].astype(BF16), bias,
                    m_ref, l_ref, acc_ref)

    @pl.when(p == n_pages)
    def _():
        t = lax.broadcasted_iota(I32, (128, 128), 0) % n_new
        bias = jnp.where(lane <= t, 0.0, NEG)
        _paged_step(q_ref[...], knew_ref[...], vnew_ref[...], bias, m_ref, l_ref, acc_ref)
        _paged_finish(z_ref, o_ref, l_ref, acc_ref, B_KV)


def _moba_sample(pt_flat, cache_k, cache_v, knew, vnew, qbd, kmean, z, nseq, n_pages, n_new):
    page = lambda b, p, pt: (pt[b * n_pages + jnp.minimum(p, n_pages - 1)], 0, 0)
    seq3 = lambda b, p, pt: (b, 0, 0)
    grid_spec = pltpu.PrefetchScalarGridSpec(
        num_scalar_prefetch=1,
        grid=(nseq, n_pages + 1),
        in_specs=[pl.BlockSpec((None, PAGE, 1024), page),
                  pl.BlockSpec((None, PAGE, 1024), page),
                  pl.BlockSpec((None, PAGE, 1024), seq3),
                  pl.BlockSpec((None, PAGE, 1024), seq3),
                  pl.BlockSpec((None, 128, 1024), seq3),
                  pl.BlockSpec((None, 128, 1024), seq3),
                  pl.BlockSpec((None, 128, 128), seq3)],
        out_specs=pl.BlockSpec((None, 128, 128), seq3),
        scratch_shapes=[pltpu.VMEM((128, 128), F32), pltpu.VMEM((128, 128), F32),
                        pltpu.VMEM((128, 1024), F32), pltpu.VMEM((128, 128), F32)],
    )
    return pl.pallas_call(
        functools.partial(_moba_sample_kernel, n_pages=n_pages,
                          n_blocks=n_pages * PAGE // MOBA_BLOCK, n_new=n_new),
        out_shape=jax.ShapeDtypeStruct((nseq, 128, 128), BF16),
        grid_spec=grid_spec,
        compiler_params=_cp(("arbitrary", "arbitrary")),
        name="moba_sample",
    )(pt_flat, cache_k, cache_v, knew, vnew, qbd, kmean, z)


def _rope_tables(pos):
    half = HEAD_DIM // 2
    inv_freq = ROPE_THETA ** (-jnp.arange(half, dtype=F32) / half)
    ang = pos.astype(F32)[:, None] * inv_freq[None, :]
    cos, sin = jnp.cos(ang), jnp.sin(ang)
    return jnp.concatenate([cos, cos], axis=-1), jnp.concatenate([-sin, sin], axis=-1)


def _pack_w_in(w):
    offs = [0]
    for wd in (2048, 1024, 1024, 4096, 128, 32, 2048, 2048, 1024, 1024, 2048, 8192):
        offs.append(offs[-1] + wd)
    seg = lambda i: w[:, offs[i]:offs[i + 1]]
    main = jnp.concatenate([seg(0), seg(1), seg(2), seg(3), seg(6), seg(7), seg(8), seg(9), seg(10),
                            seg(11)], axis=1).astype(BF16)
    small = jnp.concatenate([seg(4), seg(5), jnp.zeros((w.shape[0], N_SMALL - 160), w.dtype)],
                            axis=1).astype(BF16)
    return main, small


def _to_grt(u, nseq, n_new, n_groups):
    u = u.reshape(nseq, n_new, n_groups, 2, HEAD_DIM)
    return jnp.transpose(u, (0, 2, 3, 1, 4)).reshape(nseq, n_groups * 2 * n_new, HEAD_DIM)


def _from_grt(u, nseq, n_new, n_groups):
    u = u.reshape(nseq, n_groups, 2, n_new, HEAD_DIM)
    return jnp.transpose(u, (0, 3, 1, 2, 4)).reshape(nseq * n_new, n_groups * 2 * HEAD_DIM)


def _block_diag_q(q, nseq, n_new, n_groups):
    qg = _to_grt(q, nseq, n_new, n_groups).reshape(nseq, n_groups, 2 * n_new, 1, HEAD_DIM)
    eye = jnp.eye(n_groups, dtype=q.dtype).reshape(1, n_groups, 1, n_groups, 1)
    return (qg * eye).reshape(nseq, n_groups * 2 * n_new, n_groups * HEAD_DIM)


def _pad_rows(u, nseq, n_new):
    u = u.reshape(nseq, n_new, u.shape[-1])
    return jnp.pad(u, ((0, 0), (0, PAGE - n_new), (0, 0)))


def kernel(x_prompt, x_sample, cache_a_k, cache_a_v, cache_idx_k, cache_b_k, cache_b_v, page_table,
           norm_gain, w_in, q_norm_a, k_norm_a, q_norm_b, k_norm_b, w_out):
    nb, seq, _ = x_prompt.shape
    nseq, n_new, _ = x_sample.shape
    n_pages = page_table.shape[1]
    past = n_pages * PAGE
    n_pool = cache_a_k.shape[1]
    assert norm_gain.shape[0] == 1 and seq % TQ == 0 and 16 * n_new == 128
    assert n_pages * PAGE // MOBA_BLOCK <= 128 and seq // MOBA_BLOCK <= 128

    w_main, w_small = _pack_w_in(w_in[0])
    w_o = w_out[0].astype(BF16)
    gains = (q_norm_a, k_norm_a, q_norm_b, k_norm_b)

    xp = x_prompt.reshape(nb * seq, D_MODEL)
    xs = x_sample.reshape(nseq * n_new, D_MODEL)
    ms = xs.shape[0]

    hp = _rmsnorm(xp, norm_gain, 512)
    hs = _rmsnorm(xs, norm_gain, ms)
    proj_p = _matmul(hp, w_main, 1024, 1024, "proj_prompt")
    small_p = _matmul(hp, w_small, 1024, N_SMALL, "proj_small_prompt")
    proj_s = _matmul(hs, w_main, ms, 1024, "proj_sample")
    small_s = _matmul(hs, w_small, ms, N_SMALL, "proj_small_sample")

    cos_p, sin_p = _rope_tables(jnp.arange(seq))
    cos_s, sin_s = _rope_tables(past + jnp.arange(ms) % n_new)
    (qa_p, ka32_p, ka_p, va32_p, va_p, iq_p, ik32_p, ik_p, iw_p,
     qb_p, kb32_p, kb_p, vb32_p, vb_p, kmean_p) = _post(proj_p, small_p, cos_p, sin_p, *gains)
    (qa_s, ka32_s, ka_s, va32_s, va_s, iq_s, ik32_s, ik_s, iw_s,
     qb_s, kb32_s, kb_s, vb32_s, vb_s, _) = _post(proj_s, small_s, cos_s, sin_s, *gains)

    bra_p = _dsa_prompt(iq_p, iw_p, ik_p, qa_p, ka_p, va_p, proj_p, nb, seq)
    nblk = seq // MOBA_BLOCK
    km_p = jnp.pad(kmean_p.reshape(nb, nblk, 1024), ((0, 0), (0, 128 - nblk), (0, 0)))
    brb_p = _moba_prompt(qb_p, kb_p, vb_p, km_p, proj_p, nb, seq)
    y_p = _merge(xp, bra_p, brb_p, w_o, proj_p, 512, 512)

    pt_flat = page_table.reshape(-1).astype(I32)
    c_idx = cache_idx_k.reshape(n_pool, PAGE, IDX_DIM)
    c_ak = cache_a_k.reshape(n_pool, PAGE, A_KV * HEAD_DIM)
    c_av = cache_a_v.reshape(n_pool, PAGE, A_KV * HEAD_DIM)
    c_bk = cache_b_k.reshape(n_pool, PAGE, B_KV * HEAD_DIM)
    c_bv = cache_b_v.reshape(n_pool, PAGE, B_KV * HEAD_DIM)

    iq_ht = jnp.transpose(iq_s.reshape(nseq, n_new, IDX_HEADS, IDX_DIM), (0, 2, 1, 3))
    iq_ht = iq_ht.reshape(nseq, IDX_HEADS * n_new, IDX_DIM)
    scores = _idx_sample(pt_flat, c_idx, _pad_rows(ik_s, nseq, n_new), iq_ht, iw_s, nseq, n_pages, n_new)
    bias_s = _thresh(scores, past, n_new, 64)
    za = _to_grt(proj_s[:, OFF_AZ:OFF_AZ + BRANCH_W], nseq, n_new, A_KV)
    bra_s = _dsa_sample(pt_flat, c_ak, c_av, _pad_rows(ka_s, nseq, n_new), _pad_rows(va_s, nseq, n_new),
                        _block_diag_q(qa_s, nseq, n_new, A_KV), bias_s, za, nseq, n_pages, n_new)
    bra_s = _from_grt(bra_s, nseq, n_new, A_KV)

    km_s = _kmean_sample(pt_flat, c_bk, nseq, n_pages).reshape(nseq, -1, 1024)
    km_s = jnp.pad(km_s, ((0, 0), (0, 128 - km_s.shape[1]), (0, 0)))
    zb = _to_grt(proj_s[:, OFF_BZ:OFF_BZ + BRANCH_W], nseq, n_new, B_KV)
    brb_s = _moba_sample(pt_flat, c_bk, c_bv, _pad_rows(kb_s, nseq, n_new), _pad_rows(vb_s, nseq, n_new),
                         _block_diag_q(qb_s, nseq, n_new, B_KV), km_s, zb, nseq, n_pages, n_new)
    brb_s = _from_grt(brb_s, nseq, n_new, B_KV)
    y_s = _merge(xs, bra_s, brb_s, w_o, proj_s, ms, 512)

    kv = lambda u, b, t: u.reshape(1, b, t, 8, HEAD_DIM)
    ix = lambda u, b, t: u.reshape(1, b, t, IDX_DIM)
    return (y_p.reshape(nb, seq, D_MODEL), y_s.reshape(nseq, n_new, D_MODEL),
            kv(ka32_p, nb, seq), kv(va32_p, nb, seq), ix(ik32_p, nb, seq),
            kv(kb32_p, nb, seq), kv(vb32_p, nb, seq),
            kv(ka32_s, nseq, n_new), kv(va32_s, nseq, n_new), ix(ik32_s, nseq, n_new),
            kv(kb32_s, nseq, n_new), kv(vb32_s, nseq, n_new))
```

```python
import functools

import jax
import jax.numpy as jnp
from jax import lax
from jax.experimental import pallas as pl
from jax.experimental.pallas import tpu as pltpu

F32 = jnp.float32
BF16 = jnp.bfloat16
I32 = jnp.int32

D_MODEL = 4096
HEAD_DIM = 128
A_HEADS = 16
A_KV = 8
IDX_HEADS = 32
IDX_DIM = 128
IDX_TOPK = 256
B_HEADS = 16
B_KV = 8
MOBA_BLOCK = 256
MOBA_TOPK = 3
BRANCH_W = 2048
ROPE_THETA = 10000.0
EPS = 1e-6
PAGE = 128

OFF_AQ, OFF_AK, OFF_AV, OFF_IQ, OFF_AZ = 0, 2048, 3072, 4096, 8192
OFF_BQ, OFF_BK, OFF_BV, OFF_BZ, OFF_GATE = 10240, 12288, 13312, 14336, 16384
N_MAIN = 24576
N_SMALL = 256

NEG = -1e30
INT_MIN = -(2 ** 31)
KEY_NEG_INF = -2139095041
TQ = 256
TK = 512
IDX_PAGES = 8
KV_PAGES = 4
VMEM_LIMIT = 56 * 1024 * 1024

_NT = (((1,), (1,)), ((), ()))


def _cp(sem, vmem=None):
    return pltpu.CompilerParams(dimension_semantics=sem, vmem_limit_bytes=vmem)


def _rmsnorm_kernel(x_ref, g_ref, o_ref):
    x = x_ref[...]
    ms = jnp.mean(x * x, axis=-1, keepdims=True)
    o_ref[...] = (x * lax.rsqrt(ms + EPS) * g_ref[...]).astype(BF16)


def _rmsnorm(x, gain, tm):
    m = x.shape[0]
    return pl.pallas_call(
        _rmsnorm_kernel,
        out_shape=jax.ShapeDtypeStruct((m, D_MODEL), BF16),
        grid=(m // tm,),
        in_specs=[pl.BlockSpec((tm, D_MODEL), lambda i: (i, 0)),
                  pl.BlockSpec((1, D_MODEL), lambda i: (0, 0))],
        out_specs=pl.BlockSpec((tm, D_MODEL), lambda i: (i, 0)),
        compiler_params=_cp(("parallel",)),
        name="rmsnorm",
    )(x, gain)


def _matmul_kernel(x_ref, w_ref, o_ref):
    o_ref[...] = jnp.dot(x_ref[...], w_ref[...], preferred_element_type=F32)


def _matmul(x, w, tm, tn, name):
    m, k = x.shape
    n = w.shape[1]
    return pl.pallas_call(
        _matmul_kernel,
        out_shape=jax.ShapeDtypeStruct((m, n), F32),
        grid=(m // tm, n // tn),
        in_specs=[pl.BlockSpec((tm, k), lambda i, j: (i, 0)),
                  pl.BlockSpec((k, tn), lambda i, j: (0, j))],
        out_specs=pl.BlockSpec((tm, tn), lambda i, j: (i, j)),
        compiler_params=_cp(("parallel", "parallel"), VMEM_LIMIT),
        name=name,
    )(x, w)


def _rope(y, cos, sin):
    return y * cos + pltpu.roll(y, 64, 1) * sin


def _head_norm(x, gain):
    ms = jnp.mean(x * x, axis=-1, keepdims=True)
    return x * lax.rsqrt(ms + EPS) * gain


def _post_kernel(pa_ref, piq_ref, pbq_ref, pbkv_ref, ps_ref, cos_ref, sin_ref,
                 qna_ref, kna_ref, qnb_ref, knb_ref,
                 qa_ref, ka32_ref, ka_ref, va32_ref, va_ref, iq_ref, ik32_ref, ik_ref, iw_ref,
                 qb_ref, kb32_ref, kb_ref, vb32_ref, vb_ref, kmean_ref):
    cos = cos_ref[...]
    sin = sin_ref[...]
    scale = HEAD_DIM ** -0.5
    hs = lambda h: slice(h * HEAD_DIM, (h + 1) * HEAD_DIM)

    qna, kna, qnb, knb = qna_ref[...], kna_ref[...], qnb_ref[...], knb_ref[...]
    for h in range(A_HEADS):
        q = _rope(_head_norm(pa_ref[:, hs(h)], qna), cos, sin)
        qa_ref[:, hs(h)] = (q * scale).astype(BF16)
    for h in range(A_KV):
        k = _rope(_head_norm(pa_ref[:, hs(A_HEADS + h)], kna), cos, sin)
        ka32_ref[:, hs(h)] = k
        ka_ref[:, hs(h)] = k.astype(BF16)
        v = pa_ref[:, hs(A_HEADS + A_KV + h)]
        va32_ref[:, hs(h)] = v
        va_ref[:, hs(h)] = v.astype(BF16)
    for h in range(IDX_HEADS):
        iq_ref[:, hs(h)] = _rope(piq_ref[:, hs(h)], cos, sin).astype(BF16)
    ik = _rope(ps_ref[:, 0:IDX_DIM], cos, sin)
    ik32_ref[...] = ik
    ik_ref[...] = ik.astype(BF16)
    iw_ref[...] = ps_ref[:, IDX_DIM:2 * IDX_DIM] * (IDX_HEADS ** -0.5 * IDX_DIM ** -0.5)
    for h in range(B_HEADS):
        q = _rope(_head_norm(pbq_ref[:, hs(h)], qnb), cos, sin)
        qb_ref[:, hs(h)] = (q * scale).astype(BF16)
    for h in range(B_KV):
        k = _rope(_head_norm(pbkv_ref[:, hs(h)], knb), cos, sin)
        kb32_ref[:, hs(h)] = k
        kb_ref[:, hs(h)] = k.astype(BF16)
        kmean_ref[0, :, hs(h)] = jnp.mean(k, axis=0, keepdims=True)
        v = pbkv_ref[:, hs(B_KV + h)]
        vb32_ref[:, hs(h)] = v
        vb_ref[:, hs(h)] = v.astype(BF16)


def _post(proj, small, cos, sin, qna, kna, qnb, knb):
    m = proj.shape[0]
    r = TQ
    nt = cos.shape[0] // r
    row = lambda c: (lambda i: (i, c))
    gain = pl.BlockSpec((1, HEAD_DIM), lambda i: (0, 0))
    tab = pl.BlockSpec((r, HEAD_DIM), lambda i: (i % nt, 0))
    sds = jax.ShapeDtypeStruct
    kv32 = sds((m, 1024), F32)
    kvbf = sds((m, 1024), BF16)
    kvspec = pl.BlockSpec((r, 1024), row(0))
    out_shape = (sds((m, 2048), BF16), kv32, kvbf, kv32, kvbf,
                 sds((m, 4096), BF16), sds((m, 128), F32), sds((m, 128), BF16), sds((m, 128), F32),
                 sds((m, 2048), BF16), kv32, kvbf, kv32, kvbf,
                 sds((m // r, 1, 1024), F32))
    s128 = pl.BlockSpec((r, 128), row(0))
    out_specs = (pl.BlockSpec((r, 2048), row(0)), kvspec, kvspec, kvspec, kvspec,
                 pl.BlockSpec((r, 4096), row(0)), s128, s128, s128,
                 pl.BlockSpec((r, 2048), row(0)), kvspec, kvspec, kvspec, kvspec,
                 pl.BlockSpec((1, 1, 1024), lambda i: (i, 0, 0)))
    return pl.pallas_call(
        _post_kernel,
        out_shape=out_shape,
        grid=(m // r,),
        in_specs=[pl.BlockSpec((r, 4096), row(0)),
                  pl.BlockSpec((r, 4096), row(1)),
                  pl.BlockSpec((r, 2048), row(OFF_BQ // 2048)),
                  pl.BlockSpec((r, 2048), row(OFF_BK // 2048)),
                  pl.BlockSpec((r, N_SMALL), row(0)),
                  tab, tab, gain, gain, gain, gain],
        out_specs=out_specs,
        compiler_params=_cp(("parallel",), VMEM_LIMIT),
        name="post",
    )(proj, proj, proj, proj, small, cos, sin, qna, kna, qnb, knb)


def _key_to_float(key):
    bits = jnp.where(key >= 0, key, key ^ jnp.int32(0x7FFFFFFF))
    return lax.bitcast_convert_type(bits, F32)


def _kth_largest_key(count_ge, shape):
    def bit_body(b, thr):
        cand = thr + lax.shift_left(jnp.int32(1), 31 - b)
        n = count_ge(_key_to_float(cand))
        return jnp.where(cand <= KEY_NEG_INF, cand, jnp.where(n >= IDX_TOPK, cand, thr))

    return lax.fori_loop(0, 32, bit_body, jnp.full(shape, INT_MIN, I32))


def _silu(z):
    return z * jax.nn.sigmoid(z)


def _rep(x, n, axis):
    return jnp.concatenate([x] * n, axis=axis)


ROW_BLOCK = 64


def _attend_init(m_ref, l_ref, acc_ref):
    m_ref[...] = jnp.full(m_ref.shape, NEG, F32)
    l_ref[...] = jnp.zeros(l_ref.shape, F32)
    acc_ref[...] = jnp.zeros(acc_ref.shape, F32)


def _attend_tile(q2, k, v, bias_rows, s_ref, p_ref, m_ref, l_ref, acc_ref):
    s_ref[...] = lax.dot_general(q2, k, _NT, preferred_element_type=F32)
    reps = k.shape[0] // 128
    for b in range(q2.shape[0] // ROW_BLOCK):
        rs = slice(b * ROW_BLOCK, (b + 1) * ROW_BLOCK)
        s = s_ref[rs, :] + bias_rows(b)
        m_prev = m_ref[rs, :]
        m_new = jnp.maximum(m_prev, jnp.max(s, axis=-1, keepdims=True))
        alpha = jnp.exp(m_prev - m_new)
        p = jnp.exp(s - _rep(m_new, reps, 1))
        l_ref[rs, :] = alpha * l_ref[rs, :] + jnp.sum(p, axis=-1, keepdims=True)
        m_ref[rs, :] = m_new
        acc_ref[rs, :] = acc_ref[rs, :] * alpha
        p_ref[rs, :] = p.astype(BF16)
    acc_ref[...] += jnp.dot(p_ref[...], v, preferred_element_type=F32)


def _dsa_prompt_kernel(iq_ref, iw_ref, ik_ref, qa_ref, ka_ref, va_ref, az_ref, o_ref,
                       wb_ref, sc_ref, bias_ref, cut_ref, s_ref, p_ref, m_ref, l_ref, acc_ref):
    i = pl.program_id(1)
    g = pl.program_id(2)
    n_tiles = i + 1
    half = TQ // 2
    lane = lax.broadcasted_iota(I32, (TQ, 128), 1)
    rowg = lax.broadcasted_iota(I32, (TQ, 128), 0) + i * TQ

    @pl.when(g == 0)
    def _select():
        for h in range(IDX_HEADS):
            wb_ref[h] = jnp.broadcast_to(iw_ref[:, h:h + 1], (TQ, 128))

        def score_tile(j, _):
            ks = ik_ref[pl.ds(pl.multiple_of(j * TQ, TQ), TQ), :]
            for rh in range(2):
                rs = slice(rh * half, (rh + 1) * half)
                acc0 = jnp.zeros((half, 128), F32)
                acc1 = jnp.zeros((half, 128), F32)
                for h in range(IDX_HEADS):
                    lg = lax.dot_general(iq_ref[rs, h * 128:(h + 1) * 128], ks, _NT,
                                         preferred_element_type=F32)
                    lg = jnp.maximum(lg, 0.0)
                    w = wb_ref[h, rs, :]
                    acc0 = acc0 + lg[:, :128] * w
                    acc1 = acc1 + lg[:, 128:] * w
                colg = lax.broadcasted_iota(I32, (half, 128), 1) + j * TQ
                rg = lax.broadcasted_iota(I32, (half, 128), 0) + (i * TQ + rh * half)
                sc_ref[j, rs, 0:128] = jnp.where(colg <= rg, acc0, -jnp.inf)
                sc_ref[j, rs, 128:256] = jnp.where(colg + 128 <= rg, acc1, -jnp.inf)
            return 0

        lax.fori_loop(0, n_tiles, score_tile, 0)

        def count(pred):
            def body(j, cnt):
                t = sc_ref[j]
                colg = lane + j * TQ
                return (cnt + jnp.where(pred(t[:, :128], colg), 1.0, 0.0)
                        + jnp.where(pred(t[:, 128:], colg + 128), 1.0, 0.0))
            cnt = lax.fori_loop(0, n_tiles, body, jnp.zeros((TQ, 128), F32))
            return jnp.sum(cnt, axis=-1, keepdims=True)

        thr = _key_to_float(_kth_largest_key(lambda c: count(lambda t, _: t >= c), (TQ, 128)))
        need = IDX_TOPK - count(lambda t, _: t > thr)
        n_tie = count(lambda t, _: t == thr)

        cut_ref[...] = jnp.full((TQ, 128), 4096, I32)

        @pl.when(jnp.max(n_tie - need) > 0.0)
        def _ties():
            def cut_body(b, cut):
                cand = cut + lax.shift_left(jnp.int32(1), 11 - b)
                n = count(lambda t, c: jnp.where(t == thr, c, 4096) < cand)
                return jnp.where(n < need, cand, cut)
            cut_ref[...] = lax.fori_loop(0, 12, cut_body, jnp.zeros((TQ, 128), I32))

        cut = cut_ref[...]

        def bias_tile(j, _):
            t = sc_ref[j]
            for hh in range(2):
                th = t[:, hh * 128:(hh + 1) * 128]
                colg = lane + (j * TQ + hh * 128)
                tie = jnp.where(th == thr, jnp.where(colg <= cut, 0.0, NEG), NEG)
                picked = jnp.where(th > thr, 0.0, tie)
                bias_ref[j, :, hh * 128:(hh + 1) * 128] = jnp.where(colg <= rowg, picked, NEG)
            return 0

        lax.fori_loop(0, n_tiles, bias_tile, 0)

        @pl.when(i % 2 == 0)
        def _pad_tile():
            bias_ref[i + 1] = jnp.full((TQ, TQ), NEG, F32)

    q2 = jnp.concatenate([qa_ref[:, 0:128], qa_ref[:, 128:256]], axis=0)
    _attend_init(m_ref, l_ref, acc_ref)

    def body(jj, _):
        off = pl.multiple_of(jj * TK, TK)

        def bias_rows(b):
            rs = slice(b * ROW_BLOCK % TQ, b * ROW_BLOCK % TQ + ROW_BLOCK)
            return jnp.concatenate([bias_ref[2 * jj, rs, :], bias_ref[2 * jj + 1, rs, :]], axis=1)

        _attend_tile(q2, ka_ref[pl.ds(off, TK), :], va_ref[pl.ds(off, TK), :], bias_rows,
                     s_ref, p_ref, m_ref, l_ref, acc_ref)
        return 0

    lax.fori_loop(0, (i + 2) // 2, body, 0)
    for r in range(2):
        cs = slice(r * 128, (r + 1) * 128)
        rs = slice(r * TQ, (r + 1) * TQ)
        o_ref[:, cs] = (acc_ref[rs, :] / l_ref[rs, :] * _silu(az_ref[:, cs])).astype(BF16)


def _dsa_prompt(iq, iw, ik, qa, ka, va, proj, nb, seq):
    nq = seq // TQ
    m = nb * seq
    rowt = lambda b, i, g: b * nq + i
    return pl.pallas_call(
        _dsa_prompt_kernel,
        out_shape=jax.ShapeDtypeStruct((m, BRANCH_W), BF16),
        grid=(nb, nq, A_KV),
        in_specs=[pl.BlockSpec((TQ, 4096), lambda b, i, g: (rowt(b, i, g), 0)),
                  pl.BlockSpec((TQ, 128), lambda b, i, g: (rowt(b, i, g), 0)),
                  pl.BlockSpec((seq, 128), lambda b, i, g: (b, 0)),
                  pl.BlockSpec((TQ, 256), lambda b, i, g: (rowt(b, i, g), g)),
                  pl.BlockSpec((seq, 128), lambda b, i, g: (b, g)),
                  pl.BlockSpec((seq, 128), lambda b, i, g: (b, g)),
                  pl.BlockSpec((TQ, 256), lambda b, i, g: (rowt(b, i, g), OFF_AZ // 256 + g))],
        out_specs=pl.BlockSpec((TQ, 256), lambda b, i, g: (rowt(b, i, g), g)),
        scratch_shapes=[pltpu.VMEM((IDX_HEADS, TQ, 128), F32),
                        pltpu.VMEM((seq // TQ, TQ, TQ), F32),
                        pltpu.VMEM((seq // TQ, TQ, TQ), F32),
                        pltpu.VMEM((TQ, 128), I32),
                        pltpu.VMEM((2 * TQ, TK), F32), pltpu.VMEM((2 * TQ, TK), BF16),
                        pltpu.VMEM((2 * TQ, 128), F32), pltpu.VMEM((2 * TQ, 128), F32),
                        pltpu.VMEM((2 * TQ, 128), F32)],
        compiler_params=_cp(("arbitrary", "arbitrary", "arbitrary"), VMEM_LIMIT),
        name="dsa_prompt",
    )(iq, iw, ik, qa, ka, va, proj)


def _top_blocks(gate, n_valid):
    lane = lax.broadcasted_iota(I32, gate.shape, 1).astype(F32)
    valid = lane < n_valid
    rem = jnp.where(valid, gate, -jnp.inf)
    sel = jnp.zeros(gate.shape, F32)
    for _ in range(MOBA_TOPK):
        mx = jnp.max(rem, axis=-1, keepdims=True)
        first = jnp.min(jnp.where(rem == mx, lane, 128.0), axis=-1, keepdims=True)
        pick = lane == first
        sel = jnp.where(pick, 1.0, sel)
        rem = jnp.where(pick, -jnp.inf, rem)
    return jnp.where(valid, sel, 0.0)


def _block_bias(sel, lane, n):
    picked = jnp.max(jnp.where(lane == n, sel, 0.0), axis=-1, keepdims=True)
    return jnp.where(picked > 0.0, 0.0, NEG)


def _moba_prompt_kernel(qb_ref, kb_ref, vb_ref, km_ref, bz_ref, o_ref, s_ref, p_ref, m_ref, l_ref, acc_ref):
    i = pl.program_id(1)
    f = i // 2
    lane = lax.broadcasted_iota(I32, (TQ, 128), 1)
    row = lax.broadcasted_iota(I32, (TQ, TQ), 0)
    col = lax.broadcasted_iota(I32, (TQ, TQ), 1)
    own_bias = jnp.where(col <= row, 0.0, NEG)
    neg_tile = jnp.full((TQ, TQ), NEG, F32)
    km = km_ref[...].astype(BF16)

    q2 = jnp.concatenate([qb_ref[:, 0:128], qb_ref[:, 128:256]], axis=0)
    gate = lax.dot_general(q2, km, _NT, preferred_element_type=F32)
    sels = [_top_blocks(gate[r * TQ:(r + 1) * TQ], i) for r in range(2)]
    _attend_init(m_ref, l_ref, acc_ref)
    per_head = TQ // ROW_BLOCK

    def kv(jj):
        off = pl.multiple_of(jj * TK, TK)
        return kb_ref[pl.ds(off, TK), :], vb_ref[pl.ds(off, TK), :]

    def body(jj, _):
        cols = [[_block_bias(sels[r], lane, 2 * jj + u) for u in range(2)] for r in range(2)]

        def bias_rows(b):
            r, rs = b // per_head, slice(b % per_head * ROW_BLOCK, (b % per_head + 1) * ROW_BLOCK)
            return jnp.concatenate([jnp.broadcast_to(c[rs], (ROW_BLOCK, TQ)) for c in cols[r]], axis=1)

        _attend_tile(q2, *kv(jj), bias_rows, s_ref, p_ref, m_ref, l_ref, acc_ref)
        return 0

    lax.fori_loop(0, f, body, 0)

    even = i % 2 == 0
    last = []
    for r in range(2):
        past = jnp.broadcast_to(_block_bias(sels[r], lane, 2 * f), (TQ, TQ))
        last.append(jnp.concatenate([jnp.where(even, own_bias, past),
                                     jnp.where(even, neg_tile, own_bias)], axis=1))

    def last_rows(b):
        r = b // per_head
        return last[r][b % per_head * ROW_BLOCK:(b % per_head + 1) * ROW_BLOCK]

    _attend_tile(q2, *kv(f), last_rows, s_ref, p_ref, m_ref, l_ref, acc_ref)
    for r in range(2):
        cs = slice(r * 128, (r + 1) * 128)
        rs = slice(r * TQ, (r + 1) * TQ)
        o_ref[:, cs] = (acc_ref[rs, :] / l_ref[rs, :] * _silu(bz_ref[:, cs])).astype(BF16)


def _moba_prompt(qb, kb, vb, kmean, proj, nb, seq):
    nq = seq // TQ
    m = nb * seq
    rowt = lambda b, i, g: b * nq + i
    return pl.pallas_call(
        _moba_prompt_kernel,
        out_shape=jax.ShapeDtypeStruct((m, BRANCH_W), BF16),
        grid=(nb, nq, B_KV),
        in_specs=[pl.BlockSpec((TQ, 256), lambda b, i, g: (rowt(b, i, g), g)),
                  pl.BlockSpec((seq, 128), lambda b, i, g: (b, g)),
                  pl.BlockSpec((seq, 128), lambda b, i, g: (b, g)),
                  pl.BlockSpec((None, 128, 128), lambda b, i, g: (b, 0, g)),
                  pl.BlockSpec((TQ, 256), lambda b, i, g: (rowt(b, i, g), OFF_BZ // 256 + g))],
        out_specs=pl.BlockSpec((TQ, 256), lambda b, i, g: (rowt(b, i, g), g)),
        scratch_shapes=[pltpu.VMEM((2 * TQ, TK), F32), pltpu.VMEM((2 * TQ, TK), BF16),
                        pltpu.VMEM((2 * TQ, 128), F32), pltpu.VMEM((2 * TQ, 128), F32),
                        pltpu.VMEM((2 * TQ, 128), F32)],
        compiler_params=_cp(("arbitrary", "arbitrary", "arbitrary"), VMEM_LIMIT),
        name="moba_prompt",
    )(qb, kb, vb, kmean, proj)


def _merge_kernel(x_ref, a_ref, b_ref, wa_ref, wb_ref, ga_ref, gb_ref, o_ref):
    pa = jnp.dot(a_ref[...], wa_ref[...], preferred_element_type=F32)
    pb = jnp.dot(b_ref[...], wb_ref[...], preferred_element_type=F32)
    o_ref[...] = x_ref[...] + (jax.nn.sigmoid(ga_ref[...]) * pa + jax.nn.sigmoid(gb_ref[...]) * pb)


def _merge(x, bra, brb, w_out, proj, tm, tn):
    m = x.shape[0]
    ga0 = OFF_GATE // tn
    gb0 = (OFF_GATE + D_MODEL) // tn
    return pl.pallas_call(
        _merge_kernel,
        out_shape=jax.ShapeDtypeStruct((m, D_MODEL), F32),
        grid=(m // tm, D_MODEL // tn),
        in_specs=[pl.BlockSpec((tm, tn), lambda i, j: (i, j)),
                  pl.BlockSpec((tm, BRANCH_W), lambda i, j: (i, 0)),
                  pl.BlockSpec((tm, BRANCH_W), lambda i, j: (i, 0)),
                  pl.BlockSpec((None, BRANCH_W, tn), lambda i, j: (0, 0, j)),
                  pl.BlockSpec((None, BRANCH_W, tn), lambda i, j: (1, 0, j)),
                  pl.BlockSpec((tm, tn), lambda i, j: (i, ga0 + j)),
                  pl.BlockSpec((tm, tn), lambda i, j: (i, gb0 + j))],
        out_specs=pl.BlockSpec((tm, tn), lambda i, j: (i, j)),
        compiler_params=_cp(("parallel", "parallel"), VMEM_LIMIT),
        name="merge",
    )(x, bra, brb, w_out, w_out, proj, proj)


def _page_specs(block, n_pages, per_step):
    def spec(r):
        def index_map(b, p, pt):
            page = jnp.minimum(p * per_step + r, n_pages - 1)
            return (pt[b * n_pages + page],) + (0,) * (len(block) - 1)
        return pl.BlockSpec(block, index_map)
    return [spec(r) for r in range(per_step)]


def _kv_rows(refs, n_heads):
    def head(ref, g):
        return ref[pl.ds(g, PAGE, stride=n_heads), :]
    return jnp.concatenate(
        [jnp.concatenate([head(ref, g) for g in range(n_heads)], axis=1) for ref in refs], axis=0)


def _idx_sample_kernel(pt_ref, *refs, n_steps, n_new):
    kp_refs = refs[:IDX_PAGES]
    knew_ref, iq_ref, iw_ref, o_ref = refs[IDX_PAGES:]
    p = pl.program_id(1)

    def scores(keys):
        lg = lax.dot_general(iq_ref[...], keys, _NT, preferred_element_type=F32)
        acc = jnp.zeros((n_new, keys.shape[0]), F32)
        for h in range(IDX_HEADS):
            w = jnp.broadcast_to(iw_ref[:, h:h + 1], acc.shape)
            acc = acc + jnp.maximum(lg[h * n_new:(h + 1) * n_new, :], 0.0) * w
        return acc

    @pl.when(p < n_steps)
    def _():
        keys = jnp.concatenate([r[...] for r in kp_refs], axis=0).astype(BF16)
        o_ref[...] = scores(keys)

    @pl.when(p == n_steps)
    def _():
        acc = scores(knew_ref[...])
        t = lax.broadcasted_iota(I32, acc.shape, 0)
        c = lax.broadcasted_iota(I32, acc.shape, 1)
        o_ref[:, 0:PAGE] = jnp.where(c <= t, acc, -jnp.inf)
        o_ref[:, PAGE:] = jnp.full((n_new, o_ref.shape[1] - PAGE), -jnp.inf, F32)


def _idx_sample(pt_flat, cache_idx, knew_pad, iq_ht, iw, nseq, n_pages, n_new):
    n_steps = n_pages // IDX_PAGES
    width = IDX_PAGES * PAGE
    grid_spec = pltpu.PrefetchScalarGridSpec(
        num_scalar_prefetch=1,
        grid=(nseq, n_steps + 1),
        in_specs=_page_specs((None, PAGE, IDX_DIM), n_pages, IDX_PAGES) + [
            pl.BlockSpec((None, PAGE, IDX_DIM), lambda b, p, pt: (b, 0, 0)),
            pl.BlockSpec((None, IDX_HEADS * n_new, IDX_DIM), lambda b, p, pt: (b, 0, 0)),
            pl.BlockSpec((n_new, 128), lambda b, p, pt: (b, 0))],
        out_specs=pl.BlockSpec((n_new, width), lambda b, p, pt: (b, p)),
    )
    return pl.pallas_call(
        functools.partial(_idx_sample_kernel, n_steps=n_steps, n_new=n_new),
        out_shape=jax.ShapeDtypeStruct((nseq * n_new, (n_steps + 1) * width), F32),
        grid_spec=grid_spec,
        compiler_params=_cp(("arbitrary", "arbitrary")),
        name="idx_sample",
    )(pt_flat, *([cache_idx] * IDX_PAGES), knew_pad, iq_ht, iw)


def _thresh_kernel(s_ref, o_ref, *, n_past, n_new):
    rows, width = s_ref.shape
    s = s_ref[...]
    col = lax.broadcasted_iota(I32, (rows, width), 1)
    t_new = lax.broadcasted_iota(I32, (rows, width), 0) % n_new

    def count(mask):
        return jnp.sum(jnp.where(mask, 1.0, 0.0), axis=-1, keepdims=True)

    thr = _key_to_float(_kth_largest_key(lambda c: count(s >= c), (rows, 1)))
    need = IDX_TOPK - count(s > thr)
    tie_col = jnp.where(s == thr, col, 2 ** 14)

    def cut_body(b, cut):
        cand = cut + lax.shift_left(jnp.int32(1), 13 - b)
        return jnp.where(count(tie_col < cand) < need, cand, cut)

    cut = lax.fori_loop(0, 14, cut_body, jnp.zeros((rows, 1), I32))
    tie = jnp.where(s == thr, jnp.where(col <= cut, 0.0, NEG), NEG)
    picked = jnp.where(s > thr, 0.0, tie)
    o_ref[...] = jnp.where(col <= n_past + t_new, picked, NEG)


def _thresh(scores, n_past, n_new, rows):
    m, width = scores.shape
    return pl.pallas_call(
        functools.partial(_thresh_kernel, n_past=n_past, n_new=n_new),
        out_shape=jax.ShapeDtypeStruct((m, width), F32),
        grid=(m // rows,),
        in_specs=[pl.BlockSpec((rows, width), lambda i: (i, 0))],
        out_specs=pl.BlockSpec((rows, width), lambda i: (i, 0)),
        compiler_params=_cp(("parallel",), VMEM_LIMIT),
        name="thresh_sample",
    )(scores)


def _diag_blocks(x, n_groups):
    rows = x.shape[0] // n_groups
    return jnp.concatenate(
        [x[g * rows:(g + 1) * rows, g * 128:(g + 1) * 128] for g in range(n_groups)], axis=0)


def _dsa_sample_kernel(pt_ref, *refs, n_steps):
    kp_refs, vp_refs = refs[:KV_PAGES], refs[KV_PAGES:2 * KV_PAGES]
    (knew_ref, vnew_ref, q_ref, bias_ref, bias_new_ref, z_ref, o_ref,
     m_ref, l_ref, acc_ref) = refs[2 * KV_PAGES:]
    p = pl.program_id(1)

    @pl.when(p == 0)
    def _():
        m_ref[...] = jnp.full(m_ref.shape, NEG, F32)
        l_ref[...] = jnp.zeros(l_ref.shape, F32)
        acc_ref[...] = jnp.zeros(acc_ref.shape, F32)

    def step(k, v, bias8):
        s = lax.dot_general(q_ref[...], k, _NT, preferred_element_type=F32)
        s = s + _rep(bias8, 128 // bias8.shape[0], 0)
        m_prev = m_ref[...]
        m_new = jnp.maximum(m_prev, jnp.max(s, axis=-1, keepdims=True))
        alpha = jnp.exp(m_prev - m_new)
        pr = jnp.exp(s - _rep(m_new, s.shape[1] // 128, 1))
        l_ref[...] = alpha * l_ref[...] + jnp.sum(pr, axis=-1, keepdims=True)
        m_ref[...] = m_new
        pv = jnp.dot(pr.astype(BF16), v, preferred_element_type=F32)
        acc_ref[...] = acc_ref[...] * alpha + _diag_blocks(pv, A_KV)

    @pl.when(p < n_steps)
    def _():
        step(_kv_rows(kp_refs, A_KV).astype(BF16), _kv_rows(vp_refs, A_KV).astype(BF16), bias_ref[...])

    @pl.when(p == n_steps)
    def _():
        step(knew_ref[...], vnew_ref[...], bias_new_ref[...])
        o_ref[...] = (acc_ref[...] / l_ref[...] * _silu(z_ref[...])).astype(BF16)


def _dsa_sample(pt_flat, cache_k, cache_v, knew, vnew, qbd, bias, z, nseq, n_pages, n_new):
    n_steps = n_pages // KV_PAGES
    seq3 = lambda b, p, pt: (b, 0, 0)
    page_block = (None, PAGE * A_KV, HEAD_DIM)
    grid_spec = pltpu.PrefetchScalarGridSpec(
        num_scalar_prefetch=1,
        grid=(nseq, n_steps + 1),
        in_specs=_page_specs(page_block, n_pages, KV_PAGES) + _page_specs(page_block, n_pages, KV_PAGES) + [
            pl.BlockSpec((None, PAGE, 1024), seq3),
            pl.BlockSpec((None, PAGE, 1024), seq3),
            pl.BlockSpec((None, 128, 1024), seq3),
            pl.BlockSpec((n_new, KV_PAGES * PAGE), lambda b, p, pt: (b, jnp.minimum(p, n_steps - 1))),
            pl.BlockSpec((n_new, PAGE), lambda b, p, pt: (b, n_pages)),
            pl.BlockSpec((None, 128, 128), seq3)],
        out_specs=pl.BlockSpec((None, 128, 128), seq3),
        scratch_shapes=[pltpu.VMEM((128, 128), F32), pltpu.VMEM((128, 128), F32),
                        pltpu.VMEM((128, 128), F32)],
    )
    return pl.pallas_call(
        functools.partial(_dsa_sample_kernel, n_steps=n_steps),
        out_shape=jax.ShapeDtypeStruct((nseq, 128, 128), BF16),
        grid_spec=grid_spec,
        compiler_params=_cp(("arbitrary", "arbitrary"), VMEM_LIMIT),
        name="dsa_sample",
    )(pt_flat, *([cache_k] * KV_PAGES), *([cache_v] * KV_PAGES), knew, vnew, qbd, bias, bias, z)


def _moba_sample_kernel(pt_ref, *refs, n_steps, n_blocks, n_new):
    kp_refs, vp_refs = refs[:KV_PAGES], refs[KV_PAGES:2 * KV_PAGES]
    (knew_ref, vnew_ref, q_ref, z_ref, o_ref,
     m_ref, l_ref, acc_ref, km_ref) = refs[2 * KV_PAGES:]
    p = pl.program_id(1)
    ppb = MOBA_BLOCK // PAGE
    blocks_per_step = KV_PAGES // ppb

    def partial_softmax(k, v, bias):
        s = lax.dot_general(q_ref[...], k, _NT, preferred_element_type=F32)
        if bias is not None:
            s = s + bias
        m = jnp.max(s, axis=-1, keepdims=True)
        pr = jnp.exp(s - m)
        l = jnp.sum(pr, axis=-1, keepdims=True)
        pv = jnp.dot(pr.astype(BF16), v, preferred_element_type=F32)
        shape = (128, 128)
        return jnp.broadcast_to(m, shape), jnp.broadcast_to(l, shape), _diag_blocks(pv, B_KV)

    @pl.when(p == 0)
    def _():
        km_ref[...] = jnp.zeros(km_ref.shape, F32)

    @pl.when(p < n_steps)
    def _():
        for u in range(blocks_per_step):
            n = p * blocks_per_step + u
            k32 = _kv_rows(kp_refs[u * ppb:(u + 1) * ppb], B_KV)
            v = _kv_rows(vp_refs[u * ppb:(u + 1) * ppb], B_KV).astype(BF16)
            km_ref[pl.ds(n, 1), :] = jnp.sum(k32, axis=0, keepdims=True) * (1.0 / MOBA_BLOCK)
            m_ref[n], l_ref[n], acc_ref[n] = partial_softmax(k32.astype(BF16), v, None)

    @pl.when(p == n_steps)
    def _():
        gate = lax.dot_general(q_ref[...], km_ref[...].astype(BF16), _NT, preferred_element_type=F32)
        sel = _top_blocks(gate, n_blocks)
        lane = lax.broadcasted_iota(I32, (128, 128), 1)
        t = lax.broadcasted_iota(I32, (128, 128), 0) % n_new
        m_own, l_own, acc_own = partial_softmax(knew_ref[...], vnew_ref[...],
                                                jnp.where(lane <= t, 0.0, NEG))
        picked = [jnp.broadcast_to(sel[:, n:n + 1], (128, 128)) > 0.0 for n in range(n_blocks)]
        m_all = m_own
        for n in range(n_blocks):
            m_all = jnp.maximum(m_all, jnp.where(picked[n], m_ref[n], NEG))
        w = jnp.exp(m_own - m_all)
        num = w * acc_own
        den = w * l_own
        for n in range(n_blocks):
            w = jnp.where(picked[n], jnp.exp(m_ref[n] - m_all), 0.0)
            num = num + w * acc_ref[n]
            den = den + w * l_ref[n]
        o_ref[...] = (num / den * _silu(z_ref[...])).astype(BF16)


def _moba_sample(pt_flat, cache_k, cache_v, knew, vnew, qbd, z, nseq, n_pages, n_new):
    n_steps = n_pages // KV_PAGES
    n_blocks = n_pages * PAGE // MOBA_BLOCK
    seq3 = lambda b, p, pt: (b, 0, 0)
    page_block = (None, PAGE * B_KV, HEAD_DIM)
    grid_spec = pltpu.PrefetchScalarGridSpec(
        num_scalar_prefetch=1,
        grid=(nseq, n_steps + 1),
        in_specs=_page_specs(page_block, n_pages, KV_PAGES) + _page_specs(page_block, n_pages, KV_PAGES) + [
            pl.BlockSpec((None, PAGE, 1024), seq3),
            pl.BlockSpec((None, PAGE, 1024), seq3),
            pl.BlockSpec((None, 128, 1024), seq3),
            pl.BlockSpec((None, 128, 128), seq3)],
        out_specs=pl.BlockSpec((None, 128, 128), seq3),
        scratch_shapes=[pltpu.VMEM((n_blocks, 128, 128), F32), pltpu.VMEM((n_blocks, 128, 128), F32),
                        pltpu.VMEM((n_blocks, 128, 128), F32), pltpu.VMEM((128, 1024), F32)],
    )
    return pl.pallas_call(
        functools.partial(_moba_sample_kernel, n_steps=n_steps, n_blocks=n_blocks, n_new=n_new),
        out_shape=jax.ShapeDtypeStruct((nseq, 128, 128), BF16),
        grid_spec=grid_spec,
        compiler_params=_cp(("arbitrary", "arbitrary"), VMEM_LIMIT),
        name="moba_sample",
    )(pt_flat, *([cache_k] * KV_PAGES), *([cache_v] * KV_PAGES), knew, vnew, qbd, z)


def _rope_tables(pos):
    half = HEAD_DIM // 2
    inv_freq = ROPE_THETA ** (-jnp.arange(half, dtype=F32) / half)
    ang = pos.astype(F32)[:, None] * inv_freq[None, :]
    cos, sin = jnp.cos(ang), jnp.sin(ang)
    return jnp.concatenate([cos, cos], axis=-1), jnp.concatenate([-sin, sin], axis=-1)


def _pack_w_in(w):
    offs = [0]
    for wd in (2048, 1024, 1024, 4096, 128, 32, 2048, 2048, 1024, 1024, 2048, 8192):
        offs.append(offs[-1] + wd)
    seg = lambda i: w[:, offs[i]:offs[i + 1]]
    main = jnp.concatenate([seg(0), seg(1), seg(2), seg(3), seg(6), seg(7), seg(8), seg(9), seg(10),
                            seg(11)], axis=1).astype(BF16)
    small = jnp.concatenate([seg(4), seg(5), jnp.zeros((w.shape[0], N_SMALL - 160), w.dtype)],
                            axis=1).astype(BF16)
    return main, small


def _to_grt(u, nseq, n_new, n_groups):
    u = u.reshape(nseq, n_new, n_groups, 2, HEAD_DIM)
    return jnp.transpose(u, (0, 2, 3, 1, 4)).reshape(nseq, n_groups * 2 * n_new, HEAD_DIM)


def _from_grt(u, nseq, n_new, n_groups):
    u = u.reshape(nseq, n_groups, 2, n_new, HEAD_DIM)
    return jnp.transpose(u, (0, 3, 1, 2, 4)).reshape(nseq * n_new, n_groups * 2 * HEAD_DIM)


def _block_diag_q(q, nseq, n_new, n_groups):
    qg = _to_grt(q, nseq, n_new, n_groups).reshape(nseq, n_groups, 2 * n_new, 1, HEAD_DIM)
    eye = jnp.eye(n_groups, dtype=q.dtype).reshape(1, n_groups, 1, n_groups, 1)
    return (qg * eye).reshape(nseq, n_groups * 2 * n_new, n_groups * HEAD_DIM)


def _pad_rows(u, nseq, n_new):
    u = u.reshape(nseq, n_new, u.shape[-1])
    return jnp.pad(u, ((0, 0), (0, PAGE - n_new), (0, 0)))


def kernel(x_prompt, x_sample, cache_a_k, cache_a_v, cache_idx_k, cache_b_k, cache_b_v, page_table,
           norm_gain, w_in, q_norm_a, k_norm_a, q_norm_b, k_norm_b, w_out):
    nb, seq, _ = x_prompt.shape
    nseq, n_new, _ = x_sample.shape
    n_pages = page_table.shape[1]
    past = n_pages * PAGE
    n_pool = cache_a_k.shape[1]
    assert norm_gain.shape[0] == 1 and seq % TK == 0 and 16 * n_new == 128
    assert n_pages % IDX_PAGES == 0 and n_pages % KV_PAGES == 0
    assert n_pages * PAGE // MOBA_BLOCK <= 128 and seq // MOBA_BLOCK <= 128

    w_main, w_small = _pack_w_in(w_in[0])
    w_o = w_out[0].astype(BF16)
    gains = (q_norm_a, k_norm_a, q_norm_b, k_norm_b)

    xp = x_prompt.reshape(nb * seq, D_MODEL)
    xs = x_sample.reshape(nseq * n_new, D_MODEL)
    ms = xs.shape[0]

    hp = _rmsnorm(xp, norm_gain, 512)
    hs = _rmsnorm(xs, norm_gain, ms)
    proj_p = _matmul(hp, w_main, 1024, 1024, "proj_prompt")
    small_p = _matmul(hp, w_small, 1024, N_SMALL, "proj_small_prompt")
    proj_s = _matmul(hs, w_main, ms, 1024, "proj_sample")
    small_s = _matmul(hs, w_small, ms, N_SMALL, "proj_small_sample")

    cos_p, sin_p = _rope_tables(jnp.arange(seq))
    cos_s, sin_s = _rope_tables(past + jnp.arange(ms) % n_new)
    (qa_p, ka32_p, ka_p, va32_p, va_p, iq_p, ik32_p, ik_p, iw_p,
     qb_p, kb32_p, kb_p, vb32_p, vb_p, kmean_p) = _post(proj_p, small_p, cos_p, sin_p, *gains)
    (qa_s, ka32_s, ka_s, va32_s, va_s, iq_s, ik32_s, ik_s, iw_s,
     qb_s, kb32_s, kb_s, vb32_s, vb_s, _) = _post(proj_s, small_s, cos_s, sin_s, *gains)

    bra_p = _dsa_prompt(iq_p, iw_p, ik_p, qa_p, ka_p, va_p, proj_p, nb, seq)
    nblk = seq // MOBA_BLOCK
    km_p = jnp.pad(kmean_p.reshape(nb, nblk, 1024), ((0, 0), (0, 128 - nblk), (0, 0)))
    brb_p = _moba_prompt(qb_p, kb_p, vb_p, km_p, proj_p, nb, seq)
    y_p = _merge(xp, bra_p, brb_p, w_o, proj_p, 512, 512)

    pt_flat = page_table.reshape(-1).astype(I32)
    c_idx = cache_idx_k.reshape(n_pool, PAGE, IDX_DIM)
    pool = lambda c: c.reshape(n_pool, PAGE * c.shape[-2], HEAD_DIM)

    iq_ht = jnp.transpose(iq_s.reshape(nseq, n_new, IDX_HEADS, IDX_DIM), (0, 2, 1, 3))
    iq_ht = iq_ht.reshape(nseq, IDX_HEADS * n_new, IDX_DIM)
    scores = _idx_sample(pt_flat, c_idx, _pad_rows(ik_s, nseq, n_new), iq_ht, iw_s, nseq, n_pages, n_new)
    bias_s = _thresh(scores, past, n_new, 64)
    za = _to_grt(proj_s[:, OFF_AZ:OFF_AZ + BRANCH_W], nseq, n_new, A_KV)
    bra_s = _dsa_sample(pt_flat, pool(cache_a_k), pool(cache_a_v),
                        _pad_rows(ka_s, nseq, n_new), _pad_rows(va_s, nseq, n_new),
                        _block_diag_q(qa_s, nseq, n_new, A_KV), bias_s, za, nseq, n_pages, n_new)
    bra_s = _from_grt(bra_s, nseq, n_new, A_KV)

    zb = _to_grt(proj_s[:, OFF_BZ:OFF_BZ + BRANCH_W], nseq, n_new, B_KV)
    brb_s = _moba_sample(pt_flat, pool(cache_b_k), pool(cache_b_v),
                         _pad_rows(kb_s, nseq, n_new), _pad_rows(vb_s, nseq, n_new),
                         _block_diag_q(qb_s, nseq, n_new, B_KV), zb, nseq, n_pages, n_new)
    brb_s = _from_grt(brb_s, nseq, n_new, B_KV)
    y_s = _merge(xs, bra_s, brb_s, w_o, proj_s, ms, 512)

    kv = lambda u, b, t: u.reshape(1, b, t, 8, HEAD_DIM)
    ix = lambda u, b, t: u.reshape(1, b, t, IDX_DIM)
    return (y_p.reshape(nb, seq, D_MODEL), y_s.reshape(nseq, n_new, D_MODEL),
            kv(ka32_p, nb, seq), kv(va32_p, nb, seq), ix(ik32_p, nb, seq),
            kv(kb32_p, nb, seq), kv(vb32_p, nb, seq),
            kv(ka32_s, nseq, n_new), kv(va32_s, nseq, n_new), ix(ik32_s, nseq, n_new),
            kv(kb32_s, nseq, n_new), kv(vb32_s, nseq, n_new))
```

```python
import functools

import jax
import jax.numpy as jnp
from jax import lax
from jax.experimental import pallas as pl
from jax.experimental.pallas import tpu as pltpu

F32 = jnp.float32
BF16 = jnp.bfloat16
I32 = jnp.int32

D_MODEL = 4096
HEAD_DIM = 128
A_HEADS = 16
A_KV = 8
IDX_HEADS = 32
IDX_DIM = 128
IDX_TOPK = 256
B_HEADS = 16
B_KV = 8
MOBA_BLOCK = 256
MOBA_TOPK = 3
BRANCH_W = 2048
ROPE_THETA = 10000.0
EPS = 1e-6
PAGE = 128

OFF_AQ, OFF_AK, OFF_AV, OFF_IQ, OFF_AZ = 0, 2048, 3072, 4096, 8192
OFF_BQ, OFF_BK, OFF_BV, OFF_BZ, OFF_GATE = 10240, 12288, 13312, 14336, 16384
N_MAIN = 24576
N_SMALL = 256

NEG = -1e30
INT_MIN = -(2 ** 31)
KEY_NEG_INF = -2139095041
TQ = 256
TK = 512
IDX_PAGES = 8
KV_PAGES = 8
VMEM_LIMIT = 56 * 1024 * 1024

_NT = (((1,), (1,)), ((), ()))


def _cp(sem, vmem=None):
    return pltpu.CompilerParams(dimension_semantics=sem, vmem_limit_bytes=vmem)


def _rmsnorm_kernel(x_ref, g_ref, o_ref):
    x = x_ref[...]
    ms = jnp.mean(x * x, axis=-1, keepdims=True)
    o_ref[...] = (x * lax.rsqrt(ms + EPS) * g_ref[...]).astype(BF16)


def _rmsnorm2_kernel(xa_ref, xb_ref, g_ref, o_ref, *, na):
    i = pl.program_id(0)

    @pl.when(i < na)
    def _():
        _rmsnorm_kernel(xa_ref, g_ref, o_ref)

    @pl.when(i >= na)
    def _():
        _rmsnorm_kernel(xb_ref, g_ref, o_ref)


def _rmsnorm(xa, xb, gain, tm):
    na, nb = xa.shape[0] // tm, xb.shape[0] // tm
    return pl.pallas_call(
        functools.partial(_rmsnorm2_kernel, na=na),
        out_shape=jax.ShapeDtypeStruct(((na + nb) * tm, D_MODEL), BF16),
        grid=(na + nb,),
        in_specs=[pl.BlockSpec((tm, D_MODEL), lambda i: (jnp.minimum(i, na - 1), 0)),
                  pl.BlockSpec((tm, D_MODEL), lambda i: (jnp.maximum(i - na, 0), 0)),
                  pl.BlockSpec((1, D_MODEL), lambda i: (0, 0))],
        out_specs=pl.BlockSpec((tm, D_MODEL), lambda i: (i, 0)),
        compiler_params=_cp(("arbitrary",)),
        name="rmsnorm",
    )(xa, xb, gain)


def _matmul_kernel(x_ref, w_ref, o_ref):
    o_ref[...] = jnp.dot(x_ref[...], w_ref[...], preferred_element_type=F32)


def _matmul(x, w, tm, tn, name):
    m, k = x.shape
    n = w.shape[1]
    return pl.pallas_call(
        _matmul_kernel,
        out_shape=jax.ShapeDtypeStruct((m, n), F32),
        grid=(m // tm, n // tn),
        in_specs=[pl.BlockSpec((tm, k), lambda i, j: (i, 0)),
                  pl.BlockSpec((k, tn), lambda i, j: (0, j))],
        out_specs=pl.BlockSpec((tm, tn), lambda i, j: (i, j)),
        compiler_params=_cp(("parallel", "parallel"), VMEM_LIMIT),
        name=name,
    )(x, w)


def _rope(y, cos, sin):
    return y * cos + pltpu.roll(y, 64, 1) * sin


def _head_norm(x, gain):
    ms = jnp.mean(x * x, axis=-1, keepdims=True)
    return x * lax.rsqrt(ms + EPS) * gain


def _post_kernel(pa_ref, piq_ref, pbq_ref, pbkv_ref, ps_ref, cos_ref, sin_ref,
                 qna_ref, kna_ref, qnb_ref, knb_ref,
                 qa_ref, ka32_ref, ka_ref, va32_ref, va_ref, iq_ref, ik32_ref, ik_ref, iw_ref,
                 qb_ref, kb32_ref, kb_ref, vb32_ref, vb_ref, kmean_ref):
    cos = cos_ref[...]
    sin = sin_ref[...]
    scale = HEAD_DIM ** -0.5
    hs = lambda h: slice(h * HEAD_DIM, (h + 1) * HEAD_DIM)

    qna, kna, qnb, knb = qna_ref[...], kna_ref[...], qnb_ref[...], knb_ref[...]
    for h in range(A_HEADS):
        q = _rope(_head_norm(pa_ref[:, hs(h)], qna), cos, sin)
        qa_ref[:, hs(h)] = (q * scale).astype(BF16)
    for h in range(A_KV):
        k = _rope(_head_norm(pa_ref[:, hs(A_HEADS + h)], kna), cos, sin)
        ka32_ref[:, hs(h)] = k
        ka_ref[:, hs(h)] = k.astype(BF16)
        v = pa_ref[:, hs(A_HEADS + A_KV + h)]
        va32_ref[:, hs(h)] = v
        va_ref[:, hs(h)] = v.astype(BF16)
    for h in range(IDX_HEADS):
        iq_ref[:, hs(h)] = _rope(piq_ref[:, hs(h)], cos, sin).astype(BF16)
    ik = _rope(ps_ref[:, 0:IDX_DIM], cos, sin)
    ik32_ref[...] = ik
    ik_ref[...] = ik.astype(BF16)
    iw_ref[...] = ps_ref[:, IDX_DIM:2 * IDX_DIM] * (IDX_HEADS ** -0.5 * IDX_DIM ** -0.5)
    for h in range(B_HEADS):
        q = _rope(_head_norm(pbq_ref[:, hs(h)], qnb), cos, sin)
        qb_ref[:, hs(h)] = (q * scale).astype(BF16)
    for h in range(B_KV):
        k = _rope(_head_norm(pbkv_ref[:, hs(h)], knb), cos, sin)
        kb32_ref[:, hs(h)] = k
        kb_ref[:, hs(h)] = k.astype(BF16)
        kmean_ref[0, :, hs(h)] = jnp.mean(k, axis=0, keepdims=True)
        v = pbkv_ref[:, hs(B_KV + h)]
        vb32_ref[:, hs(h)] = v
        vb_ref[:, hs(h)] = v.astype(BF16)


def _post(proj, small, cos, sin, qna, kna, qnb, knb, m, row0):
    r = TQ
    nt = cos.shape[0] // r
    t0 = row0 // r
    row = lambda c: (lambda i: (i, c))
    src = lambda c: (lambda i: (t0 + i, c))
    gain = pl.BlockSpec((1, HEAD_DIM), lambda i: (0, 0))
    tab = pl.BlockSpec((r, HEAD_DIM), lambda i: (i % nt, 0))
    sds = jax.ShapeDtypeStruct
    kv32 = sds((m, 1024), F32)
    kvbf = sds((m, 1024), BF16)
    kvspec = pl.BlockSpec((r, 1024), row(0))
    out_shape = (sds((m, 2048), BF16), kv32, kvbf, kv32, kvbf,
                 sds((m, 4096), BF16), sds((m, 128), F32), sds((m, 128), BF16), sds((m, 128), F32),
                 sds((m, 2048), BF16), kv32, kvbf, kv32, kvbf,
                 sds((m // r, 1, 1024), F32))
    s128 = pl.BlockSpec((r, 128), row(0))
    out_specs = (pl.BlockSpec((r, 2048), row(0)), kvspec, kvspec, kvspec, kvspec,
                 pl.BlockSpec((r, 4096), row(0)), s128, s128, s128,
                 pl.BlockSpec((r, 2048), row(0)), kvspec, kvspec, kvspec, kvspec,
                 pl.BlockSpec((1, 1, 1024), lambda i: (i, 0, 0)))
    return pl.pallas_call(
        _post_kernel,
        out_shape=out_shape,
        grid=(m // r,),
        in_specs=[pl.BlockSpec((r, 4096), src(0)),
                  pl.BlockSpec((r, 4096), src(1)),
                  pl.BlockSpec((r, 2048), src(OFF_BQ // 2048)),
                  pl.BlockSpec((r, 2048), src(OFF_BK // 2048)),
                  pl.BlockSpec((r, N_SMALL), src(0)),
                  tab, tab, gain, gain, gain, gain],
        out_specs=out_specs,
        compiler_params=_cp(("parallel",), VMEM_LIMIT),
        name="post",
    )(proj, proj, proj, proj, small, cos, sin, qna, kna, qnb, knb)


def _key_to_float(key):
    bits = jnp.where(key >= 0, key, key ^ jnp.int32(0x7FFFFFFF))
    return lax.bitcast_convert_type(bits, F32)


def _kth_largest_key(count_ge, shape):
    def bit_body(b, thr):
        cand = thr + lax.shift_left(jnp.int32(1), 31 - b)
        n = count_ge(_key_to_float(cand))
        return jnp.where(cand <= KEY_NEG_INF, cand, jnp.where(n >= IDX_TOPK, cand, thr))

    return lax.fori_loop(0, 32, bit_body, jnp.full(shape, INT_MIN, I32))


def _silu(z):
    return z * jax.nn.sigmoid(z)


def _rep(x, n, axis):
    return jnp.concatenate([x] * n, axis=axis)


ROW_BLOCK = 64


def _attend_init(m_ref, l_ref, acc_ref):
    m_ref[...] = jnp.full(m_ref.shape, NEG, F32)
    l_ref[...] = jnp.zeros(l_ref.shape, F32)
    acc_ref[...] = jnp.zeros(acc_ref.shape, F32)


def _attend_tile(q2, k, v, bias_rows, s_ref, p_ref, m_ref, l_ref, acc_ref):
    s_ref[...] = lax.dot_general(q2, k, _NT, preferred_element_type=F32)
    reps = k.shape[0] // 128
    for b in range(q2.shape[0] // ROW_BLOCK):
        rs = slice(b * ROW_BLOCK, (b + 1) * ROW_BLOCK)
        s = s_ref[rs, :] + bias_rows(b)
        m_prev = m_ref[rs, :]
        m_new = jnp.maximum(m_prev, jnp.max(s, axis=-1, keepdims=True))
        alpha = jnp.exp(m_prev - m_new)
        p = jnp.exp(s - _rep(m_new, reps, 1))
        l_ref[rs, :] = alpha * l_ref[rs, :] + jnp.sum(p, axis=-1, keepdims=True)
        m_ref[rs, :] = m_new
        acc_ref[rs, :] = acc_ref[rs, :] * alpha
        p_ref[rs, :] = p.astype(BF16)
    acc_ref[...] += jnp.dot(p_ref[...], v, preferred_element_type=F32)


def _dsa_prompt_kernel(iq_ref, iw_ref, ik_ref, qa_ref, ka_ref, va_ref, az_ref, o_ref,
                       wb_ref, sc_ref, bias_ref, cut_ref, s_ref, p_ref, m_ref, l_ref, acc_ref):
    i = pl.program_id(1)
    g = pl.program_id(2)
    n_tiles = i + 1
    half = TQ // 2
    lane = lax.broadcasted_iota(I32, (TQ, 128), 1)
    rowg = lax.broadcasted_iota(I32, (TQ, 128), 0) + i * TQ

    @pl.when(g == 0)
    def _select():
        for h in range(IDX_HEADS):
            wb_ref[h] = jnp.broadcast_to(iw_ref[:, h:h + 1], (TQ, 128))

        def score_tile(j, _):
            ks = ik_ref[pl.ds(pl.multiple_of(j * TQ, TQ), TQ), :]
            for rh in range(2):
                rs = slice(rh * half, (rh + 1) * half)
                acc0 = jnp.zeros((half, 128), F32)
                acc1 = jnp.zeros((half, 128), F32)
                for h in range(IDX_HEADS):
                    lg = lax.dot_general(iq_ref[rs, h * 128:(h + 1) * 128], ks, _NT,
                                         preferred_element_type=F32)
                    lg = jnp.maximum(lg, 0.0)
                    w = wb_ref[h, rs, :]
                    acc0 = acc0 + lg[:, :128] * w
                    acc1 = acc1 + lg[:, 128:] * w
                colg = lax.broadcasted_iota(I32, (half, 128), 1) + j * TQ
                rg = lax.broadcasted_iota(I32, (half, 128), 0) + (i * TQ + rh * half)
                sc_ref[j, rs, 0:128] = jnp.where(colg <= rg, acc0, -jnp.inf)
                sc_ref[j, rs, 128:256] = jnp.where(colg + 128 <= rg, acc1, -jnp.inf)
            return 0

        lax.fori_loop(0, n_tiles, score_tile, 0)

        def count(pred):
            def body(j, cnt):
                t = sc_ref[j]
                colg = lane + j * TQ
                return (cnt + jnp.where(pred(t[:, :128], colg), 1.0, 0.0)
                        + jnp.where(pred(t[:, 128:], colg + 128), 1.0, 0.0))
            cnt = lax.fori_loop(0, n_tiles, body, jnp.zeros((TQ, 128), F32))
            return jnp.sum(cnt, axis=-1, keepdims=True)

        thr = _key_to_float(_kth_largest_key(lambda c: count(lambda t, _: t >= c), (TQ, 128)))
        need = IDX_TOPK - count(lambda t, _: t > thr)
        n_tie = count(lambda t, _: t == thr)

        cut_ref[...] = jnp.full((TQ, 128), 4096, I32)

        @pl.when(jnp.max(n_tie - need) > 0.0)
        def _ties():
            def cut_body(b, cut):
                cand = cut + lax.shift_left(jnp.int32(1), 11 - b)
                n = count(lambda t, c: jnp.where(t == thr, c, 4096) < cand)
                return jnp.where(n < need, cand, cut)
            cut_ref[...] = lax.fori_loop(0, 12, cut_body, jnp.zeros((TQ, 128), I32))

        cut = cut_ref[...]

        def bias_tile(j, _):
            t = sc_ref[j]
            for hh in range(2):
                th = t[:, hh * 128:(hh + 1) * 128]
                colg = lane + (j * TQ + hh * 128)
                tie = jnp.where(th == thr, jnp.where(colg <= cut, 0.0, NEG), NEG)
                picked = jnp.where(th > thr, 0.0, tie)
                bias_ref[j, :, hh * 128:(hh + 1) * 128] = jnp.where(colg <= rowg, picked, NEG)
            return 0

        lax.fori_loop(0, n_tiles, bias_tile, 0)

        @pl.when(i % 2 == 0)
        def _pad_tile():
            bias_ref[i + 1] = jnp.full((TQ, TQ), NEG, F32)

    q2 = jnp.concatenate([qa_ref[:, 0:128], qa_ref[:, 128:256]], axis=0)
    _attend_init(m_ref, l_ref, acc_ref)

    def body(jj, _):
        off = pl.multiple_of(jj * TK, TK)

        def bias_rows(b):
            rs = slice(b * ROW_BLOCK % TQ, b * ROW_BLOCK % TQ + ROW_BLOCK)
            return jnp.concatenate([bias_ref[2 * jj, rs, :], bias_ref[2 * jj + 1, rs, :]], axis=1)

        _attend_tile(q2, ka_ref[pl.ds(off, TK), :], va_ref[pl.ds(off, TK), :], bias_rows,
                     s_ref, p_ref, m_ref, l_ref, acc_ref)
        return 0

    lax.fori_loop(0, (i + 2) // 2, body, 0)
    for r in range(2):
        cs = slice(r * 128, (r + 1) * 128)
        rs = slice(r * TQ, (r + 1) * TQ)
        o_ref[:, cs] = (acc_ref[rs, :] / l_ref[rs, :] * _silu(az_ref[:, cs])).astype(BF16)


def _dsa_prompt(iq, iw, ik, qa, ka, va, proj, nb, seq):
    nq = seq // TQ
    m = nb * seq
    rowt = lambda b, i, g: b * nq + i
    return pl.pallas_call(
        _dsa_prompt_kernel,
        out_shape=jax.ShapeDtypeStruct((m, BRANCH_W), BF16),
        grid=(nb, nq, A_KV),
        in_specs=[pl.BlockSpec((TQ, 4096), lambda b, i, g: (rowt(b, i, g), 0)),
                  pl.BlockSpec((TQ, 128), lambda b, i, g: (rowt(b, i, g), 0)),
                  pl.BlockSpec((seq, 128), lambda b, i, g: (b, 0)),
                  pl.BlockSpec((TQ, 256), lambda b, i, g: (rowt(b, i, g), g)),
                  pl.BlockSpec((seq, 128), lambda b, i, g: (b, g)),
                  pl.BlockSpec((seq, 128), lambda b, i, g: (b, g)),
                  pl.BlockSpec((TQ, 256), lambda b, i, g: (rowt(b, i, g), OFF_AZ // 256 + g))],
        out_specs=pl.BlockSpec((TQ, 256), lambda b, i, g: (rowt(b, i, g), g)),
        scratch_shapes=[pltpu.VMEM((IDX_HEADS, TQ, 128), F32),
                        pltpu.VMEM((seq // TQ, TQ, TQ), F32),
                        pltpu.VMEM((seq // TQ, TQ, TQ), F32),
                        pltpu.VMEM((TQ, 128), I32),
                        pltpu.VMEM((2 * TQ, TK), F32), pltpu.VMEM((2 * TQ, TK), BF16),
                        pltpu.VMEM((2 * TQ, 128), F32), pltpu.VMEM((2 * TQ, 128), F32),
                        pltpu.VMEM((2 * TQ, 128), F32)],
        compiler_params=_cp(("arbitrary", "arbitrary", "arbitrary"), VMEM_LIMIT),
        name="dsa_prompt",
    )(iq, iw, ik, qa, ka, va, proj)


def _top_blocks(gate, n_valid):
    lane = lax.broadcasted_iota(I32, gate.shape, 1).astype(F32)
    valid = lane < n_valid
    rem = jnp.where(valid, gate, -jnp.inf)
    sel = jnp.zeros(gate.shape, F32)
    for _ in range(MOBA_TOPK):
        mx = jnp.max(rem, axis=-1, keepdims=True)
        first = jnp.min(jnp.where(rem == mx, lane, 128.0), axis=-1, keepdims=True)
        pick = lane == first
        sel = jnp.where(pick, 1.0, sel)
        rem = jnp.where(pick, -jnp.inf, rem)
    return jnp.where(valid, sel, 0.0)


def _block_bias(sel, lane, n):
    picked = jnp.max(jnp.where(lane == n, sel, 0.0), axis=-1, keepdims=True)
    return jnp.where(picked > 0.0, 0.0, NEG)


def _moba_prompt_kernel(qb_ref, kb_ref, vb_ref, km_ref, bz_ref, o_ref, s_ref, p_ref, m_ref, l_ref, acc_ref):
    i = pl.program_id(1)
    f = i // 2
    lane = lax.broadcasted_iota(I32, (TQ, 128), 1)
    row = lax.broadcasted_iota(I32, (TQ, TQ), 0)
    col = lax.broadcasted_iota(I32, (TQ, TQ), 1)
    own_bias = jnp.where(col <= row, 0.0, NEG)
    neg_tile = jnp.full((TQ, TQ), NEG, F32)
    km = km_ref[...].astype(BF16)

    q2 = jnp.concatenate([qb_ref[:, 0:128], qb_ref[:, 128:256]], axis=0)
    gate = lax.dot_general(q2, km, _NT, preferred_element_type=F32)
    sels = [_top_blocks(gate[r * TQ:(r + 1) * TQ], i) for r in range(2)]
    _attend_init(m_ref, l_ref, acc_ref)
    per_head = TQ // ROW_BLOCK

    def kv(jj):
        off = pl.multiple_of(jj * TK, TK)
        return kb_ref[pl.ds(off, TK), :], vb_ref[pl.ds(off, TK), :]

    def body(jj, _):
        cols = [[_block_bias(sels[r], lane, 2 * jj + u) for u in range(2)] for r in range(2)]

        def bias_rows(b):
            r, rs = b // per_head, slice(b % per_head * ROW_BLOCK, (b % per_head + 1) * ROW_BLOCK)
            return jnp.concatenate([jnp.broadcast_to(c[rs], (ROW_BLOCK, TQ)) for c in cols[r]], axis=1)

        _attend_tile(q2, *kv(jj), bias_rows, s_ref, p_ref, m_ref, l_ref, acc_ref)
        return 0

    lax.fori_loop(0, f, body, 0)

    even = i % 2 == 0
    last = []
    for r in range(2):
        past = jnp.broadcast_to(_block_bias(sels[r], lane, 2 * f), (TQ, TQ))
        last.append(jnp.concatenate([jnp.where(even, own_bias, past),
                                     jnp.where(even, neg_tile, own_bias)], axis=1))

    def last_rows(b):
        r = b // per_head
        return last[r][b % per_head * ROW_BLOCK:(b % per_head + 1) * ROW_BLOCK]

    _attend_tile(q2, *kv(f), last_rows, s_ref, p_ref, m_ref, l_ref, acc_ref)
    for r in range(2):
        cs = slice(r * 128, (r + 1) * 128)
        rs = slice(r * TQ, (r + 1) * TQ)
        o_ref[:, cs] = (acc_ref[rs, :] / l_ref[rs, :] * _silu(bz_ref[:, cs])).astype(BF16)


def _moba_prompt(qb, kb, vb, kmean, proj, nb, seq):
    nq = seq // TQ
    m = nb * seq
    rowt = lambda b, i, g: b * nq + i
    return pl.pallas_call(
        _moba_prompt_kernel,
        out_shape=jax.ShapeDtypeStruct((m, BRANCH_W), BF16),
        grid=(nb, nq, B_KV),
        in_specs=[pl.BlockSpec((TQ, 256), lambda b, i, g: (rowt(b, i, g), g)),
                  pl.BlockSpec((seq, 128), lambda b, i, g: (b, g)),
                  pl.BlockSpec((seq, 128), lambda b, i, g: (b, g)),
                  pl.BlockSpec((None, 128, 128), lambda b, i, g: (b, 0, g)),
                  pl.BlockSpec((TQ, 256), lambda b, i, g: (rowt(b, i, g), OFF_BZ // 256 + g))],
        out_specs=pl.BlockSpec((TQ, 256), lambda b, i, g: (rowt(b, i, g), g)),
        scratch_shapes=[pltpu.VMEM((2 * TQ, TK), F32), pltpu.VMEM((2 * TQ, TK), BF16),
                        pltpu.VMEM((2 * TQ, 128), F32), pltpu.VMEM((2 * TQ, 128), F32),
                        pltpu.VMEM((2 * TQ, 128), F32)],
        compiler_params=_cp(("arbitrary", "arbitrary", "arbitrary"), VMEM_LIMIT),
        name="moba_prompt",
    )(qb, kb, vb, kmean, proj)


def _merge_kernel(x_ref, a_ref, b_ref, wa_ref, wb_ref, ga_ref, gb_ref, o_ref):
    pa = jnp.dot(a_ref[...], wa_ref[...], preferred_element_type=F32)
    pb = jnp.dot(b_ref[...], wb_ref[...], preferred_element_type=F32)
    o_ref[...] = x_ref[...] + (jax.nn.sigmoid(ga_ref[...]) * pa + jax.nn.sigmoid(gb_ref[...]) * pb)


def _merge(x, bra, brb, w_out, proj, tm, tn, row0):
    m = x.shape[0]
    t0 = row0 // tm
    ga0 = OFF_GATE // tn
    gb0 = (OFF_GATE + D_MODEL) // tn
    return pl.pallas_call(
        _merge_kernel,
        out_shape=jax.ShapeDtypeStruct((m, D_MODEL), F32),
        grid=(m // tm, D_MODEL // tn),
        in_specs=[pl.BlockSpec((tm, tn), lambda i, j: (i, j)),
                  pl.BlockSpec((tm, BRANCH_W), lambda i, j: (i, 0)),
                  pl.BlockSpec((tm, BRANCH_W), lambda i, j: (i, 0)),
                  pl.BlockSpec((None, BRANCH_W, tn), lambda i, j: (0, 0, j)),
                  pl.BlockSpec((None, BRANCH_W, tn), lambda i, j: (1, 0, j)),
                  pl.BlockSpec((tm, tn), lambda i, j: (t0 + i, ga0 + j)),
                  pl.BlockSpec((tm, tn), lambda i, j: (t0 + i, gb0 + j))],
        out_specs=pl.BlockSpec((tm, tn), lambda i, j: (i, j)),
        compiler_params=_cp(("parallel", "parallel"), VMEM_LIMIT),
        name="merge",
    )(x, bra, brb, w_out, w_out, proj, proj)


def _page_specs(block, n_pages, per_step):
    def spec(r):
        def index_map(b, p, pt):
            page = jnp.minimum(p * per_step + r, n_pages - 1)
            return (pt[b * n_pages + page],) + (0,) * (len(block) - 1)
        return pl.BlockSpec(block, index_map)
    return [spec(r) for r in range(per_step)]


def _new_rows(ref):
    n_new, width = ref.shape
    return jnp.concatenate([ref[...], jnp.zeros((PAGE - n_new, width), F32)], axis=0).astype(BF16)


def _kv_rows(refs, n_heads):
    def head(ref, g):
        return ref[pl.ds(g, PAGE, stride=n_heads), :]
    return jnp.concatenate(
        [jnp.concatenate([head(ref, g) for g in range(n_heads)], axis=1) for ref in refs], axis=0)


def _idx_sample_kernel(pt_ref, *refs, n_steps, n_new):
    kp_refs = refs[:IDX_PAGES]
    knew_ref, iq_ref, iw_ref, o_ref = refs[IDX_PAGES:]
    p = pl.program_id(1)

    def scores(keys):
        lg = lax.dot_general(iq_ref[...], keys, _NT, preferred_element_type=F32)
        acc = jnp.zeros((n_new, keys.shape[0]), F32)
        for h in range(IDX_HEADS):
            w = jnp.broadcast_to(iw_ref[:, h:h + 1], acc.shape)
            acc = acc + jnp.maximum(lg[h * n_new:(h + 1) * n_new, :], 0.0) * w
        return acc

    @pl.when(p < n_steps)
    def _():
        keys = jnp.concatenate([r[...] for r in kp_refs], axis=0).astype(BF16)
        o_ref[...] = scores(keys)

    @pl.when(p == n_steps)
    def _():
        acc = scores(_new_rows(knew_ref))
        t = lax.broadcasted_iota(I32, acc.shape, 0)
        c = lax.broadcasted_iota(I32, acc.shape, 1)
        o_ref[:, 0:PAGE] = jnp.where(c <= t, acc, -jnp.inf)
        o_ref[:, PAGE:] = jnp.full((n_new, o_ref.shape[1] - PAGE), -jnp.inf, F32)


def _idx_sample(pt_flat, cache_idx, knew_pad, iq_ht, iw, nseq, n_pages, n_new):
    n_steps = n_pages // IDX_PAGES
    width = IDX_PAGES * PAGE
    grid_spec = pltpu.PrefetchScalarGridSpec(
        num_scalar_prefetch=1,
        grid=(nseq, n_steps + 1),
        in_specs=_page_specs((None, PAGE, IDX_DIM), n_pages, IDX_PAGES) + [
            pl.BlockSpec((n_new, IDX_DIM), lambda b, p, pt: (b, 0)),
            pl.BlockSpec((None, IDX_HEADS * n_new, IDX_DIM), lambda b, p, pt: (b, 0, 0)),
            pl.BlockSpec((n_new, 128), lambda b, p, pt: (b, 0))],
        out_specs=pl.BlockSpec((n_new, width), lambda b, p, pt: (b, p)),
    )
    return pl.pallas_call(
        functools.partial(_idx_sample_kernel, n_steps=n_steps, n_new=n_new),
        out_shape=jax.ShapeDtypeStruct((nseq * n_new, (n_steps + 1) * width), F32),
        grid_spec=grid_spec,
        compiler_params=_cp(("arbitrary", "arbitrary")),
        name="idx_sample",
    )(pt_flat, *([cache_idx] * IDX_PAGES), knew_pad, iq_ht, iw)


def _thresh_kernel(s_ref, o_ref, *, n_past, n_new):
    rows, width = s_ref.shape
    s = s_ref[...]
    col = lax.broadcasted_iota(I32, (rows, width), 1)
    t_new = lax.broadcasted_iota(I32, (rows, width), 0) % n_new

    def count(mask):
        return jnp.sum(jnp.where(mask, 1.0, 0.0), axis=-1, keepdims=True)

    thr = _key_to_float(_kth_largest_key(lambda c: count(s >= c), (rows, 1)))
    need = IDX_TOPK - count(s > thr)
    tie_col = jnp.where(s == thr, col, 2 ** 14)

    def cut_body(b, cut):
        cand = cut + lax.shift_left(jnp.int32(1), 13 - b)
        return jnp.where(count(tie_col < cand) < need, cand, cut)

    cut = lax.fori_loop(0, 14, cut_body, jnp.zeros((rows, 1), I32))
    tie = jnp.where(s == thr, jnp.where(col <= cut, 0.0, NEG), NEG)
    picked = jnp.where(s > thr, 0.0, tie)
    o_ref[...] = jnp.where(col <= n_past + t_new, picked, NEG)


def _thresh(scores, n_past, n_new, rows):
    m, width = scores.shape
    return pl.pallas_call(
        functools.partial(_thresh_kernel, n_past=n_past, n_new=n_new),
        out_shape=jax.ShapeDtypeStruct((m, width), F32),
        grid=(m // rows,),
        in_specs=[pl.BlockSpec((rows, width), lambda i: (i, 0))],
        out_specs=pl.BlockSpec((rows, width), lambda i: (i, 0)),
        compiler_params=_cp(("parallel",), VMEM_LIMIT),
        name="thresh_sample",
    )(scores)


def _diag_blocks(x, n_groups):
    rows = x.shape[0] // n_groups
    return jnp.concatenate(
        [x[g * rows:(g + 1) * rows, g * 128:(g + 1) * 128] for g in range(n_groups)], axis=0)


def _dsa_sample_kernel(pt_ref, *refs, n_steps):
    kp_refs, vp_refs = refs[:KV_PAGES], refs[KV_PAGES:2 * KV_PAGES]
    (knew_ref, vnew_ref, q_ref, bias_ref, bias_new_ref, z_ref, o_ref,
     m_ref, l_ref, acc_ref) = refs[2 * KV_PAGES:]
    p = pl.program_id(1)

    @pl.when(p == 0)
    def _():
        m_ref[...] = jnp.full(m_ref.shape, NEG, F32)
        l_ref[...] = jnp.zeros(l_ref.shape, F32)
        acc_ref[...] = jnp.zeros(acc_ref.shape, F32)

    def step(k, v, bias8):
        s = lax.dot_general(q_ref[...], k, _NT, preferred_element_type=F32)
        s = s + _rep(bias8, 128 // bias8.shape[0], 0)
        m_prev = m_ref[...]
        m_new = jnp.maximum(m_prev, jnp.max(s, axis=-1, keepdims=True))
        alpha = jnp.exp(m_prev - m_new)
        pr = jnp.exp(s - _rep(m_new, s.shape[1] // 128, 1))
        l_ref[...] = alpha * l_ref[...] + jnp.sum(pr, axis=-1, keepdims=True)
        m_ref[...] = m_new
        pv = jnp.dot(pr.astype(BF16), v, preferred_element_type=F32)
        acc_ref[...] = acc_ref[...] * alpha + _diag_blocks(pv, A_KV)

    @pl.when(p < n_steps)
    def _():
        step(_kv_rows(kp_refs, A_KV).astype(BF16), _kv_rows(vp_refs, A_KV).astype(BF16), bias_ref[...])

    @pl.when(p == n_steps)
    def _():
        step(_new_rows(knew_ref), _new_rows(vnew_ref), bias_new_ref[...])
        o_ref[...] = (acc_ref[...] / l_ref[...] * _silu(z_ref[...])).astype(BF16)


def _dsa_sample(pt_flat, cache_k, cache_v, knew, vnew, qbd, bias, z, nseq, n_pages, n_new):
    n_steps = n_pages // KV_PAGES
    seq3 = lambda b, p, pt: (b, 0, 0)
    page_block = (None, PAGE * A_KV, HEAD_DIM)
    grid_spec = pltpu.PrefetchScalarGridSpec(
        num_scalar_prefetch=1,
        grid=(nseq, n_steps + 1),
        in_specs=_page_specs(page_block, n_pages, KV_PAGES) + _page_specs(page_block, n_pages, KV_PAGES) + [
            pl.BlockSpec((n_new, 1024), lambda b, p, pt: (b, 0)),
            pl.BlockSpec((n_new, 1024), lambda b, p, pt: (b, 0)),
            pl.BlockSpec((None, 128, 1024), seq3),
            pl.BlockSpec((n_new, KV_PAGES * PAGE), lambda b, p, pt: (b, jnp.minimum(p, n_steps - 1))),
            pl.BlockSpec((n_new, PAGE), lambda b, p, pt: (b, n_pages)),
            pl.BlockSpec((None, 128, 128), seq3)],
        out_specs=pl.BlockSpec((None, 128, 128), seq3),
        scratch_shapes=[pltpu.VMEM((128, 128), F32), pltpu.VMEM((128, 128), F32),
                        pltpu.VMEM((128, 128), F32)],
    )
    return pl.pallas_call(
        functools.partial(_dsa_sample_kernel, n_steps=n_steps),
        out_shape=jax.ShapeDtypeStruct((nseq, 128, 128), BF16),
        grid_spec=grid_spec,
        compiler_params=_cp(("arbitrary", "arbitrary"), VMEM_LIMIT),
        name="dsa_sample",
    )(pt_flat, *([cache_k] * KV_PAGES), *([cache_v] * KV_PAGES), knew, vnew, qbd, bias, bias, z)


def _moba_sample_kernel(pt_ref, *refs, n_steps, n_blocks, n_new):
    kp_refs, vp_refs = refs[:KV_PAGES], refs[KV_PAGES:2 * KV_PAGES]
    (knew_ref, vnew_ref, q_ref, z_ref, o_ref,
     m_ref, l_ref, acc_ref, km_ref) = refs[2 * KV_PAGES:]
    p = pl.program_id(1)
    ppb = MOBA_BLOCK // PAGE
    blocks_per_step = KV_PAGES // ppb

    def partial_softmax(k, v, bias):
        s = lax.dot_general(q_ref[...], k, _NT, preferred_element_type=F32)
        if bias is not None:
            s = s + bias
        m = jnp.max(s, axis=-1, keepdims=True)
        pr = jnp.exp(s - m)
        l = jnp.sum(pr, axis=-1, keepdims=True)
        pv = jnp.dot(pr.astype(BF16), v, preferred_element_type=F32)
        shape = (128, 128)
        return jnp.broadcast_to(m, shape), jnp.broadcast_to(l, shape), _diag_blocks(pv, B_KV)

    @pl.when(p == 0)
    def _():
        km_ref[...] = jnp.zeros(km_ref.shape, F32)

    @pl.when(p < n_steps)
    def _():
        for u in range(blocks_per_step):
            n = p * blocks_per_step + u
            k32 = _kv_rows(kp_refs[u * ppb:(u + 1) * ppb], B_KV)
            v = _kv_rows(vp_refs[u * ppb:(u + 1) * ppb], B_KV).astype(BF16)
            km_ref[pl.ds(n, 1), :] = jnp.sum(k32, axis=0, keepdims=True) * (1.0 / MOBA_BLOCK)
            m_ref[n], l_ref[n], acc_ref[n] = partial_softmax(k32.astype(BF16), v, None)

    @pl.when(p == n_steps)
    def _():
        gate = lax.dot_general(q_ref[...], km_ref[...].astype(BF16), _NT, preferred_element_type=F32)
        sel = _top_blocks(gate, n_blocks)
        lane = lax.broadcasted_iota(I32, (128, 128), 1)
        t = lax.broadcasted_iota(I32, (128, 128), 0) % n_new
        m_own, l_own, acc_own = partial_softmax(_new_rows(knew_ref), _new_rows(vnew_ref),
                                                jnp.where(lane <= t, 0.0, NEG))
        picked = [jnp.broadcast_to(sel[:, n:n + 1], (128, 128)) > 0.0 for n in range(n_blocks)]
        m_all = m_own
        for n in range(n_blocks):
            m_all = jnp.maximum(m_all, jnp.where(picked[n], m_ref[n], NEG))
        w = jnp.exp(m_own - m_all)
        num = w * acc_own
        den = w * l_own
        for n in range(n_blocks):
            w = jnp.where(picked[n], jnp.exp(m_ref[n] - m_all), 0.0)
            num = num + w * acc_ref[n]
            den = den + w * l_ref[n]
        o_ref[...] = (num / den * _silu(z_ref[...])).astype(BF16)


def _moba_sample(pt_flat, cache_k, cache_v, knew, vnew, qbd, z, nseq, n_pages, n_new):
    n_steps = n_pages // KV_PAGES
    n_blocks = n_pages * PAGE // MOBA_BLOCK
    seq3 = lambda b, p, pt: (b, 0, 0)
    page_block = (None, PAGE * B_KV, HEAD_DIM)
    grid_spec = pltpu.PrefetchScalarGridSpec(
        num_scalar_prefetch=1,
        grid=(nseq, n_steps + 1),
        in_specs=_page_specs(page_block, n_pages, KV_PAGES) + _page_specs(page_block, n_pages, KV_PAGES) + [
            pl.BlockSpec((n_new, 1024), lambda b, p, pt: (b, 0)),
            pl.BlockSpec((n_new, 1024), lambda b, p, pt: (b, 0)),
            pl.BlockSpec((None, 128, 1024), seq3),
            pl.BlockSpec((None, 128, 128), seq3)],
        out_specs=pl.BlockSpec((None, 128, 128), seq3),
        scratch_shapes=[pltpu.VMEM((n_blocks, 128, 128), F32), pltpu.VMEM((n_blocks, 128, 128), F32),
                        pltpu.VMEM((n_blocks, 128, 128), F32), pltpu.VMEM((128, 1024), F32)],
    )
    return pl.pallas_call(
        functools.partial(_moba_sample_kernel, n_steps=n_steps, n_blocks=n_blocks, n_new=n_new),
        out_shape=jax.ShapeDtypeStruct((nseq, 128, 128), BF16),
        grid_spec=grid_spec,
        compiler_params=_cp(("arbitrary", "arbitrary"), VMEM_LIMIT),
        name="moba_sample",
    )(pt_flat, *([cache_k] * KV_PAGES), *([cache_v] * KV_PAGES), knew, vnew, qbd, z)


def _rope_tables(pos):
    half = HEAD_DIM // 2
    inv_freq = ROPE_THETA ** (-jnp.arange(half, dtype=F32) / half)
    ang = pos.astype(F32)[:, None] * inv_freq[None, :]
    cos, sin = jnp.cos(ang), jnp.sin(ang)
    return jnp.concatenate([cos, cos], axis=-1), jnp.concatenate([-sin, sin], axis=-1)


def _pack_w_in(w):
    offs = [0]
    for wd in (2048, 1024, 1024, 4096, 128, 32, 2048, 2048, 1024, 1024, 2048, 8192):
        offs.append(offs[-1] + wd)
    seg = lambda i: w[:, offs[i]:offs[i + 1]]
    main = jnp.concatenate([seg(0), seg(1), seg(2), seg(3), seg(6), seg(7), seg(8), seg(9), seg(10),
                            seg(11)], axis=1).astype(BF16)
    small = w[:, offs[4]:offs[4] + N_SMALL].astype(BF16)
    return main, small


def _to_grt(u, nseq, n_new, n_groups):
    u = u.reshape(nseq, n_new, n_groups, 2, HEAD_DIM)
    return jnp.transpose(u, (0, 2, 3, 1, 4)).reshape(nseq, n_groups * 2 * n_new, HEAD_DIM)


def _from_grt(u, nseq, n_new, n_groups):
    u = u.reshape(nseq, n_groups, 2, n_new, HEAD_DIM)
    return jnp.transpose(u, (0, 3, 1, 2, 4)).reshape(nseq * n_new, n_groups * 2 * HEAD_DIM)


def _block_diag_q(q, nseq, n_new, n_groups):
    qg = _to_grt(q, nseq, n_new, n_groups).reshape(nseq, n_groups, 2 * n_new, 1, HEAD_DIM)
    eye = jnp.eye(n_groups, dtype=q.dtype).reshape(1, n_groups, 1, n_groups, 1)
    return (qg * eye).reshape(nseq, n_groups * 2 * n_new, n_groups * HEAD_DIM)


def kernel(x_prompt, x_sample, cache_a_k, cache_a_v, cache_idx_k, cache_b_k, cache_b_v, page_table,
           norm_gain, w_in, q_norm_a, k_norm_a, q_norm_b, k_norm_b, w_out):
    nb, seq, _ = x_prompt.shape
    nseq, n_new, _ = x_sample.shape
    n_pages = page_table.shape[1]
    past = n_pages * PAGE
    n_pool = cache_a_k.shape[1]
    assert norm_gain.shape[0] == 1 and seq % TK == 0 and 16 * n_new == 128 and nseq * n_new == TQ
    assert n_pages % IDX_PAGES == 0 and n_pages % KV_PAGES == 0
    assert n_pages * PAGE // MOBA_BLOCK <= 128 and seq // MOBA_BLOCK <= 128

    w_main, w_small = _pack_w_in(w_in[0])
    w_o = w_out[0].astype(BF16)
    gains = (q_norm_a, k_norm_a, q_norm_b, k_norm_b)

    xp = x_prompt.reshape(nb * seq, D_MODEL)
    xs = x_sample.reshape(nseq * n_new, D_MODEL)
    ms = xs.shape[0]

    mp = nb * seq
    h = _rmsnorm(xp, xs, norm_gain, ms)
    tm = (mp + ms) // 8
    proj = _matmul(h, w_main, tm, 1024, "proj")
    small = _matmul(h, w_small, tm, N_SMALL, "proj_small")

    cos_p, sin_p = _rope_tables(jnp.arange(seq))
    cos_s, sin_s = _rope_tables(past + jnp.arange(ms) % n_new)
    (qa_p, ka32_p, ka_p, va32_p, va_p, iq_p, ik32_p, ik_p, iw_p,
     qb_p, kb32_p, kb_p, vb32_p, vb_p, kmean_p) = _post(proj, small, cos_p, sin_p, *gains, mp, 0)
    (qa_s, ka32_s, _, va32_s, _, iq_s, ik32_s, _, iw_s,
     qb_s, kb32_s, _, vb32_s, _, _) = _post(proj, small, cos_s, sin_s, *gains, ms, mp)

    bra_p = _dsa_prompt(iq_p, iw_p, ik_p, qa_p, ka_p, va_p, proj, nb, seq)
    nblk = seq // MOBA_BLOCK
    km_p = jnp.pad(kmean_p.reshape(nb, nblk, 1024), ((0, 0), (0, 128 - nblk), (0, 0)))
    brb_p = _moba_prompt(qb_p, kb_p, vb_p, km_p, proj, nb, seq)
    y_p = _merge(xp, bra_p, brb_p, w_o, proj, 512, 512, 0)

    pt_flat = page_table.reshape(-1).astype(I32)
    c_idx = cache_idx_k.reshape(n_pool, PAGE, IDX_DIM)
    pool = lambda c: c.reshape(n_pool, PAGE * c.shape[-2], HEAD_DIM)

    iq_ht = jnp.transpose(iq_s.reshape(nseq, n_new, IDX_HEADS, IDX_DIM), (0, 2, 1, 3))
    iq_ht = iq_ht.reshape(nseq, IDX_HEADS * n_new, IDX_DIM)
    scores = _idx_sample(pt_flat, c_idx, ik32_s, iq_ht, iw_s, nseq, n_pages, n_new)
    bias_s = _thresh(scores, past, n_new, 64)
    za = _to_grt(proj[mp:, OFF_AZ:OFF_AZ + BRANCH_W], nseq, n_new, A_KV)
    bra_s = _dsa_sample(pt_flat, pool(cache_a_k), pool(cache_a_v), ka32_s, va32_s,
                        _block_diag_q(qa_s, nseq, n_new, A_KV), bias_s, za, nseq, n_pages, n_new)
    bra_s = _from_grt(bra_s, nseq, n_new, A_KV)

    zb = _to_grt(proj[mp:, OFF_BZ:OFF_BZ + BRANCH_W], nseq, n_new, B_KV)
    brb_s = _moba_sample(pt_flat, pool(cache_b_k), pool(cache_b_v), kb32_s, vb32_s,
                         _block_diag_q(qb_s, nseq, n_new, B_KV), zb, nseq, n_pages, n_new)
    brb_s = _from_grt(brb_s, nseq, n_new, B_KV)
    y_s = _merge(xs, bra_s, brb_s, w_o, proj, ms, 512, mp)

    kv = lambda u, b, t: u.reshape(1, b, t, 8, HEAD_DIM)
    ix = lambda u, b, t: u.reshape(1, b, t, IDX_DIM)
    return (y_p.reshape(nb, seq, D_MODEL), y_s.reshape(nseq, n_new, D_MODEL),
            kv(ka32_p, nb, seq), kv(va32_p, nb, seq), ix(ik32_p, nb, seq),
            kv(kb32_p, nb, seq), kv(vb32_p, nb, seq),
            kv(ka32_s, nseq, n_new), kv(va32_s, nseq, n_new), ix(ik32_s, nseq, n_new),
            kv(kb32_s, nseq, n_new), kv(vb32_s, nseq, n_new))
```

```python
import functools

import jax
import jax.numpy as jnp
from jax import lax
from jax.experimental import pallas as pl
from jax.experimental.pallas import tpu as pltpu

F32 = jnp.float32
BF16 = jnp.bfloat16
I32 = jnp.int32

D_MODEL = 4096
HEAD_DIM = 128
A_HEADS = 16
A_KV = 8
IDX_HEADS = 32
IDX_DIM = 128
IDX_TOPK = 256
B_HEADS = 16
B_KV = 8
MOBA_BLOCK = 256
MOBA_TOPK = 3
BRANCH_W = 2048
ROPE_THETA = 10000.0
EPS = 1e-6
PAGE = 128

OFF_AQ, OFF_AK, OFF_AV, OFF_IQ, OFF_AZ = 0, 2048, 3072, 4096, 8192
OFF_BQ, OFF_BK, OFF_BV, OFF_BZ, OFF_GATE = 10240, 12288, 13312, 14336, 16384
N_MAIN = 24576
N_SMALL = 256

NEG = -1e30
INT_MIN = -(2 ** 31)
KEY_NEG_INF = -2139095041
TQ = 256
TK = 512
IDX_PAGES = 16
KV_PAGES = 8
VMEM_LIMIT = 56 * 1024 * 1024

_NT = (((1,), (1,)), ((), ()))


def _cp(sem, vmem=None):
    return pltpu.CompilerParams(dimension_semantics=sem, vmem_limit_bytes=vmem)


def _rmsnorm_kernel(x_ref, g_ref, o_ref):
    x = x_ref[...]
    ms = jnp.mean(x * x, axis=-1, keepdims=True)
    o_ref[...] = (x * lax.rsqrt(ms + EPS) * g_ref[...]).astype(BF16)


def _rmsnorm2_kernel(xa_ref, xb_ref, g_ref, o_ref, *, na):
    i = pl.program_id(0)

    @pl.when(i < na)
    def _():
        _rmsnorm_kernel(xa_ref, g_ref, o_ref)

    @pl.when(i >= na)
    def _():
        _rmsnorm_kernel(xb_ref, g_ref, o_ref)


def _rmsnorm(xa, xb, gain, tm):
    na, nb = xa.shape[0] // tm, xb.shape[0] // tm
    return pl.pallas_call(
        functools.partial(_rmsnorm2_kernel, na=na),
        out_shape=jax.ShapeDtypeStruct(((na + nb) * tm, D_MODEL), BF16),
        grid=(na + nb,),
        in_specs=[pl.BlockSpec((tm, D_MODEL), lambda i: (jnp.minimum(i, na - 1), 0)),
                  pl.BlockSpec((tm, D_MODEL), lambda i: (jnp.maximum(i - na, 0), 0)),
                  pl.BlockSpec((1, D_MODEL), lambda i: (0, 0))],
        out_specs=pl.BlockSpec((tm, D_MODEL), lambda i: (i, 0)),
        compiler_params=_cp(("arbitrary",)),
        name="rmsnorm",
    )(xa, xb, gain)


def _matmul_kernel(x_ref, w_ref, o_ref):
    o_ref[...] = jnp.dot(x_ref[...], w_ref[...], preferred_element_type=F32)


def _matmul(x, w, tm, tn, name):
    m, k = x.shape
    n = w.shape[1]
    return pl.pallas_call(
        _matmul_kernel,
        out_shape=jax.ShapeDtypeStruct((m, n), F32),
        grid=(m // tm, n // tn),
        in_specs=[pl.BlockSpec((tm, k), lambda i, j: (i, 0)),
                  pl.BlockSpec((k, tn), lambda i, j: (0, j))],
        out_specs=pl.BlockSpec((tm, tn), lambda i, j: (i, j)),
        compiler_params=_cp(("parallel", "parallel"), VMEM_LIMIT),
        name=name,
    )(x, w)


def _rope(y, cos, sin):
    return y * cos + pltpu.roll(y, 64, 1) * sin


def _head_norm(x, gain):
    ms = jnp.mean(x * x, axis=-1, keepdims=True)
    return x * lax.rsqrt(ms + EPS) * gain


def _post_kernel(pa_ref, piq_ref, pbq_ref, pbkv_ref, ps_ref, cos_ref, sin_ref,
                 qna_ref, kna_ref, qnb_ref, knb_ref,
                 qa_ref, ka32_ref, ka_ref, va32_ref, va_ref, iq_ref, ik32_ref, ik_ref, iw_ref,
                 qb_ref, kb32_ref, kb_ref, vb32_ref, vb_ref, kmean_ref):
    cos = cos_ref[...]
    sin = sin_ref[...]
    scale = HEAD_DIM ** -0.5
    hs = lambda h: slice(h * HEAD_DIM, (h + 1) * HEAD_DIM)

    qna, kna, qnb, knb = qna_ref[...], kna_ref[...], qnb_ref[...], knb_ref[...]
    for h in range(A_HEADS):
        q = _rope(_head_norm(pa_ref[:, hs(h)], qna), cos, sin)
        qa_ref[:, hs(h)] = (q * scale).astype(BF16)
    for h in range(A_KV):
        k = _rope(_head_norm(pa_ref[:, hs(A_HEADS + h)], kna), cos, sin)
        ka32_ref[:, hs(h)] = k
        ka_ref[:, hs(h)] = k.astype(BF16)
        v = pa_ref[:, hs(A_HEADS + A_KV + h)]
        va32_ref[:, hs(h)] = v
        va_ref[:, hs(h)] = v.astype(BF16)
    for h in range(IDX_HEADS):
        iq_ref[:, hs(h)] = _rope(piq_ref[:, hs(h)], cos, sin).astype(BF16)
    ik = _rope(ps_ref[:, 0:IDX_DIM], cos, sin)
    ik32_ref[...] = ik
    ik_ref[...] = ik.astype(BF16)
    iw_ref[...] = ps_ref[:, IDX_DIM:2 * IDX_DIM] * (IDX_HEADS ** -0.5 * IDX_DIM ** -0.5)
    for h in range(B_HEADS):
        q = _rope(_head_norm(pbq_ref[:, hs(h)], qnb), cos, sin)
        qb_ref[:, hs(h)] = (q * scale).astype(BF16)
    for h in range(B_KV):
        k = _rope(_head_norm(pbkv_ref[:, hs(h)], knb), cos, sin)
        kb32_ref[:, hs(h)] = k
        kb_ref[:, hs(h)] = k.astype(BF16)
        kmean_ref[0, :, hs(h)] = jnp.mean(k, axis=0, keepdims=True)
        v = pbkv_ref[:, hs(B_KV + h)]
        vb32_ref[:, hs(h)] = v
        vb_ref[:, hs(h)] = v.astype(BF16)


def _post(proj, small, cos, sin, qna, kna, qnb, knb, m, row0):
    r = TQ
    nt = cos.shape[0] // r
    t0 = row0 // r
    row = lambda c: (lambda i: (i, c))
    src = lambda c: (lambda i: (t0 + i, c))
    gain = pl.BlockSpec((1, HEAD_DIM), lambda i: (0, 0))
    tab = pl.BlockSpec((r, HEAD_DIM), lambda i: (i % nt, 0))
    sds = jax.ShapeDtypeStruct
    kv32 = sds((m, 1024), F32)
    kvbf = sds((m, 1024), BF16)
    kvspec = pl.BlockSpec((r, 1024), row(0))
    out_shape = (sds((m, 2048), BF16), kv32, kvbf, kv32, kvbf,
                 sds((m, 4096), BF16), sds((m, 128), F32), sds((m, 128), BF16), sds((m, 128), F32),
                 sds((m, 2048), BF16), kv32, kvbf, kv32, kvbf,
                 sds((m // r, 1, 1024), F32))
    s128 = pl.BlockSpec((r, 128), row(0))
    out_specs = (pl.BlockSpec((r, 2048), row(0)), kvspec, kvspec, kvspec, kvspec,
                 pl.BlockSpec((r, 4096), row(0)), s128, s128, s128,
                 pl.BlockSpec((r, 2048), row(0)), kvspec, kvspec, kvspec, kvspec,
                 pl.BlockSpec((1, 1, 1024), lambda i: (i, 0, 0)))
    return pl.pallas_call(
        _post_kernel,
        out_shape=out_shape,
        grid=(m // r,),
        in_specs=[pl.BlockSpec((r, 4096), src(0)),
                  pl.BlockSpec((r, 4096), src(1)),
                  pl.BlockSpec((r, 2048), src(OFF_BQ // 2048)),
                  pl.BlockSpec((r, 2048), src(OFF_BK // 2048)),
                  pl.BlockSpec((r, N_SMALL), src(0)),
                  tab, tab, gain, gain, gain, gain],
        out_specs=out_specs,
        compiler_params=_cp(("parallel",), VMEM_LIMIT),
        name="post",
    )(proj, proj, proj, proj, small, cos, sin, qna, kna, qnb, knb)


def _key_to_float(key):
    bits = jnp.where(key >= 0, key, key ^ jnp.int32(0x7FFFFFFF))
    return lax.bitcast_convert_type(bits, F32)


def _kth_largest_key(count_ge, shape):
    def bit_body(b, thr):
        cand = thr + lax.shift_left(jnp.int32(1), 31 - b)
        n = count_ge(_key_to_float(cand))
        return jnp.where(cand <= KEY_NEG_INF, cand, jnp.where(n >= IDX_TOPK, cand, thr))

    return lax.fori_loop(0, 32, bit_body, jnp.full(shape, INT_MIN, I32))


def _silu(z):
    return z * jax.nn.sigmoid(z)


def _rep(x, n, axis):
    return jnp.concatenate([x] * n, axis=axis)


ROW_BLOCK = 64


def _attend_init(m_ref, l_ref, acc_ref):
    m_ref[...] = jnp.full(m_ref.shape, NEG, F32)
    l_ref[...] = jnp.zeros(l_ref.shape, F32)
    acc_ref[...] = jnp.zeros(acc_ref.shape, F32)


def _attend_tile(q2, k, v, bias_rows, s_ref, p_ref, m_ref, l_ref, acc_ref):
    s_ref[...] = lax.dot_general(q2, k, _NT, preferred_element_type=F32)
    reps = k.shape[0] // 128
    for b in range(q2.shape[0] // ROW_BLOCK):
        rs = slice(b * ROW_BLOCK, (b + 1) * ROW_BLOCK)
        s = s_ref[rs, :] + bias_rows(b)
        m_prev = m_ref[rs, :]
        m_new = jnp.maximum(m_prev, jnp.max(s, axis=-1, keepdims=True))
        alpha = jnp.exp(m_prev - m_new)
        p = jnp.exp(s - _rep(m_new, reps, 1))
        l_ref[rs, :] = alpha * l_ref[rs, :] + jnp.sum(p, axis=-1, keepdims=True)
        m_ref[rs, :] = m_new
        acc_ref[rs, :] = acc_ref[rs, :] * alpha
        p_ref[rs, :] = p.astype(BF16)
    acc_ref[...] += jnp.dot(p_ref[...], v, preferred_element_type=F32)


def _dsa_prompt_kernel(iq_ref, iw_ref, ik_ref, qa_ref, ka_ref, va_ref, az_ref, o_ref,
                       wb_ref, sc_ref, bias_ref, cut_ref, s_ref, p_ref, m_ref, l_ref, acc_ref):
    i = pl.program_id(1)
    g = pl.program_id(2)
    n_tiles = i + 1
    half = TQ // 2
    lane = lax.broadcasted_iota(I32, (TQ, 128), 1)
    rowg = lax.broadcasted_iota(I32, (TQ, 128), 0) + i * TQ

    @pl.when(g == 0)
    def _select():
        for h in range(IDX_HEADS):
            wb_ref[h] = jnp.broadcast_to(iw_ref[:, h:h + 1], (TQ, 128))

        def score_tile(j, _):
            ks = ik_ref[pl.ds(pl.multiple_of(j * TQ, TQ), TQ), :]
            for rh in range(2):
                rs = slice(rh * half, (rh + 1) * half)
                acc0 = jnp.zeros((half, 128), F32)
                acc1 = jnp.zeros((half, 128), F32)
                for h in range(IDX_HEADS):
                    lg = lax.dot_general(iq_ref[rs, h * 128:(h + 1) * 128], ks, _NT,
                                         preferred_element_type=F32)
                    lg = jnp.maximum(lg, 0.0)
                    w = wb_ref[h, rs, :]
                    acc0 = acc0 + lg[:, :128] * w
                    acc1 = acc1 + lg[:, 128:] * w
                colg = lax.broadcasted_iota(I32, (half, 128), 1) + j * TQ
                rg = lax.broadcasted_iota(I32, (half, 128), 0) + (i * TQ + rh * half)
                sc_ref[j, rs, 0:128] = jnp.where(colg <= rg, acc0, -jnp.inf)
                sc_ref[j, rs, 128:256] = jnp.where(colg + 128 <= rg, acc1, -jnp.inf)
            return 0

        lax.fori_loop(0, n_tiles, score_tile, 0)

        def count(pred):
            def body(j, cnt):
                t = sc_ref[j]
                colg = lane + j * TQ
                return (cnt + jnp.where(pred(t[:, :128], colg), 1.0, 0.0)
                        + jnp.where(pred(t[:, 128:], colg + 128), 1.0, 0.0))
            cnt = lax.fori_loop(0, n_tiles, body, jnp.zeros((TQ, 128), F32))
            return jnp.sum(cnt, axis=-1, keepdims=True)

        thr = _key_to_float(_kth_largest_key(lambda c: count(lambda t, _: t >= c), (TQ, 128)))
        need = IDX_TOPK - count(lambda t, _: t > thr)
        n_tie = count(lambda t, _: t == thr)

        cut_ref[...] = jnp.full((TQ, 128), 4096, I32)

        @pl.when(jnp.max(n_tie - need) > 0.0)
        def _ties():
            def cut_body(b, cut):
                cand = cut + lax.shift_left(jnp.int32(1), 11 - b)
                n = count(lambda t, c: jnp.where(t == thr, c, 4096) < cand)
                return jnp.where(n < need, cand, cut)
            cut_ref[...] = lax.fori_loop(0, 12, cut_body, jnp.zeros((TQ, 128), I32))

        cut = cut_ref[...]

        def bias_tile(j, _):
            t = sc_ref[j]
            for hh in range(2):
                th = t[:, hh * 128:(hh + 1) * 128]
                colg = lane + (j * TQ + hh * 128)
                tie = jnp.where(th == thr, jnp.where(colg <= cut, 0.0, NEG), NEG)
                picked = jnp.where(th > thr, 0.0, tie)
                bias_ref[j, :, hh * 128:(hh + 1) * 128] = jnp.where(colg <= rowg, picked, NEG)
            return 0

        lax.fori_loop(0, n_tiles, bias_tile, 0)

        @pl.when(i % 2 == 0)
        def _pad_tile():
            bias_ref[i + 1] = jnp.full((TQ, TQ), NEG, F32)

    q2 = jnp.concatenate([qa_ref[:, 0:128], qa_ref[:, 128:256]], axis=0)
    _attend_init(m_ref, l_ref, acc_ref)

    def body(jj, _):
        off = pl.multiple_of(jj * TK, TK)

        def bias_rows(b):
            rs = slice(b * ROW_BLOCK % TQ, b * ROW_BLOCK % TQ + ROW_BLOCK)
            return jnp.concatenate([bias_ref[2 * jj, rs, :], bias_ref[2 * jj + 1, rs, :]], axis=1)

        _attend_tile(q2, ka_ref[pl.ds(off, TK), :], va_ref[pl.ds(off, TK), :], bias_rows,
                     s_ref, p_ref, m_ref, l_ref, acc_ref)
        return 0

    lax.fori_loop(0, (i + 2) // 2, body, 0)
    for r in range(2):
        cs = slice(r * 128, (r + 1) * 128)
        rs = slice(r * TQ, (r + 1) * TQ)
        o_ref[:, cs] = (acc_ref[rs, :] / l_ref[rs, :] * _silu(az_ref[:, cs])).astype(BF16)


def _dsa_prompt(iq, iw, ik, qa, ka, va, proj, nb, seq):
    nq = seq // TQ
    m = nb * seq
    rowt = lambda b, i, g: b * nq + i
    return pl.pallas_call(
        _dsa_prompt_kernel,
        out_shape=jax.ShapeDtypeStruct((m, BRANCH_W), BF16),
        grid=(nb, nq, A_KV),
        in_specs=[pl.BlockSpec((TQ, 4096), lambda b, i, g: (rowt(b, i, g), 0)),
                  pl.BlockSpec((TQ, 128), lambda b, i, g: (rowt(b, i, g), 0)),
                  pl.BlockSpec((seq, 128), lambda b, i, g: (b, 0)),
                  pl.BlockSpec((TQ, 256), lambda b, i, g: (rowt(b, i, g), g)),
                  pl.BlockSpec((seq, 128), lambda b, i, g: (b, g)),
                  pl.BlockSpec((seq, 128), lambda b, i, g: (b, g)),
                  pl.BlockSpec((TQ, 256), lambda b, i, g: (rowt(b, i, g), OFF_AZ // 256 + g))],
        out_specs=pl.BlockSpec((TQ, 256), lambda b, i, g: (rowt(b, i, g), g)),
        scratch_shapes=[pltpu.VMEM((IDX_HEADS, TQ, 128), F32),
                        pltpu.VMEM((seq // TQ, TQ, TQ), F32),
                        pltpu.VMEM((seq // TQ, TQ, TQ), F32),
                        pltpu.VMEM((TQ, 128), I32),
                        pltpu.VMEM((2 * TQ, TK), F32), pltpu.VMEM((2 * TQ, TK), BF16),
                        pltpu.VMEM((2 * TQ, 128), F32), pltpu.VMEM((2 * TQ, 128), F32),
                        pltpu.VMEM((2 * TQ, 128), F32)],
        compiler_params=_cp(("arbitrary", "arbitrary", "arbitrary"), VMEM_LIMIT),
        name="dsa_prompt",
    )(iq, iw, ik, qa, ka, va, proj)


def _top_blocks(gate, n_valid):
    lane = lax.broadcasted_iota(I32, gate.shape, 1).astype(F32)
    valid = lane < n_valid
    rem = jnp.where(valid, gate, -jnp.inf)
    sel = jnp.zeros(gate.shape, F32)
    for _ in range(MOBA_TOPK):
        mx = jnp.max(rem, axis=-1, keepdims=True)
        first = jnp.min(jnp.where(rem == mx, lane, 128.0), axis=-1, keepdims=True)
        pick = lane == first
        sel = jnp.where(pick, 1.0, sel)
        rem = jnp.where(pick, -jnp.inf, rem)
    return jnp.where(valid, sel, 0.0)


def _block_bias(sel, lane, n):
    picked = jnp.max(jnp.where(lane == n, sel, 0.0), axis=-1, keepdims=True)
    return jnp.where(picked > 0.0, 0.0, NEG)


def _moba_prompt_kernel(qb_ref, kb_ref, vb_ref, km_ref, bz_ref, o_ref, s_ref, p_ref, m_ref, l_ref, acc_ref):
    i = pl.program_id(1)
    f = i // 2
    lane = lax.broadcasted_iota(I32, (TQ, 128), 1)
    row = lax.broadcasted_iota(I32, (TQ, TQ), 0)
    col = lax.broadcasted_iota(I32, (TQ, TQ), 1)
    own_bias = jnp.where(col <= row, 0.0, NEG)
    neg_tile = jnp.full((TQ, TQ), NEG, F32)
    km = km_ref[...].astype(BF16)

    q2 = jnp.concatenate([qb_ref[:, 0:128], qb_ref[:, 128:256]], axis=0)
    gate = lax.dot_general(q2, km, _NT, preferred_element_type=F32)
    sels = [_top_blocks(gate[r * TQ:(r + 1) * TQ], i) for r in range(2)]
    _attend_init(m_ref, l_ref, acc_ref)
    per_head = TQ // ROW_BLOCK

    def kv(jj):
        off = pl.multiple_of(jj * TK, TK)
        return kb_ref[pl.ds(off, TK), :], vb_ref[pl.ds(off, TK), :]

    def body(jj, _):
        cols = [[_block_bias(sels[r], lane, 2 * jj + u) for u in range(2)] for r in range(2)]

        def bias_rows(b):
            r, rs = b // per_head, slice(b % per_head * ROW_BLOCK, (b % per_head + 1) * ROW_BLOCK)
            return jnp.concatenate([jnp.broadcast_to(c[rs], (ROW_BLOCK, TQ)) for c in cols[r]], axis=1)

        _attend_tile(q2, *kv(jj), bias_rows, s_ref, p_ref, m_ref, l_ref, acc_ref)
        return 0

    lax.fori_loop(0, f, body, 0)

    even = i % 2 == 0
    last = []
    for r in range(2):
        past = jnp.broadcast_to(_block_bias(sels[r], lane, 2 * f), (TQ, TQ))
        last.append(jnp.concatenate([jnp.where(even, own_bias, past),
                                     jnp.where(even, neg_tile, own_bias)], axis=1))

    def last_rows(b):
        r = b // per_head
        return last[r][b % per_head * ROW_BLOCK:(b % per_head + 1) * ROW_BLOCK]

    _attend_tile(q2, *kv(f), last_rows, s_ref, p_ref, m_ref, l_ref, acc_ref)
    for r in range(2):
        cs = slice(r * 128, (r + 1) * 128)
        rs = slice(r * TQ, (r + 1) * TQ)
        o_ref[:, cs] = (acc_ref[rs, :] / l_ref[rs, :] * _silu(bz_ref[:, cs])).astype(BF16)


def _moba_prompt(qb, kb, vb, kmean, proj, nb, seq):
    nq = seq // TQ
    m = nb * seq
    rowt = lambda b, i, g: b * nq + i
    return pl.pallas_call(
        _moba_prompt_kernel,
        out_shape=jax.ShapeDtypeStruct((m, BRANCH_W), BF16),
        grid=(nb, nq, B_KV),
        in_specs=[pl.BlockSpec((TQ, 256), lambda b, i, g: (rowt(b, i, g), g)),
                  pl.BlockSpec((seq, 128), lambda b, i, g: (b, g)),
                  pl.BlockSpec((seq, 128), lambda b, i, g: (b, g)),
                  pl.BlockSpec((None, 128, 128), lambda b, i, g: (b, 0, g)),
                  pl.BlockSpec((TQ, 256), lambda b, i, g: (rowt(b, i, g), OFF_BZ // 256 + g))],
        out_specs=pl.BlockSpec((TQ, 256), lambda b, i, g: (rowt(b, i, g), g)),
        scratch_shapes=[pltpu.VMEM((2 * TQ, TK), F32), pltpu.VMEM((2 * TQ, TK), BF16),
                        pltpu.VMEM((2 * TQ, 128), F32), pltpu.VMEM((2 * TQ, 128), F32),
                        pltpu.VMEM((2 * TQ, 128), F32)],
        compiler_params=_cp(("arbitrary", "arbitrary", "arbitrary"), VMEM_LIMIT),
        name="moba_prompt",
    )(qb, kb, vb, kmean, proj)


def _merge_kernel(x_ref, a_ref, b_ref, wa_ref, wb_ref, ga_ref, gb_ref, o_ref):
    pa = jnp.dot(a_ref[...], wa_ref[...], preferred_element_type=F32)
    pb = jnp.dot(b_ref[...], wb_ref[...], preferred_element_type=F32)
    o_ref[...] = x_ref[...] + (jax.nn.sigmoid(ga_ref[...]) * pa + jax.nn.sigmoid(gb_ref[...]) * pb)


def _merge(x, bra, brb, w_out, proj, tm, tn, row0):
    m = x.shape[0]
    t0 = row0 // tm
    ga0 = OFF_GATE // tn
    gb0 = (OFF_GATE + D_MODEL) // tn
    return pl.pallas_call(
        _merge_kernel,
        out_shape=jax.ShapeDtypeStruct((m, D_MODEL), F32),
        grid=(m // tm, D_MODEL // tn),
        in_specs=[pl.BlockSpec((tm, tn), lambda i, j: (i, j)),
                  pl.BlockSpec((tm, BRANCH_W), lambda i, j: (i, 0)),
                  pl.BlockSpec((tm, BRANCH_W), lambda i, j: (i, 0)),
                  pl.BlockSpec((None, BRANCH_W, tn), lambda i, j: (0, 0, j)),
                  pl.BlockSpec((None, BRANCH_W, tn), lambda i, j: (1, 0, j)),
                  pl.BlockSpec((tm, tn), lambda i, j: (t0 + i, ga0 + j)),
                  pl.BlockSpec((tm, tn), lambda i, j: (t0 + i, gb0 + j))],
        out_specs=pl.BlockSpec((tm, tn), lambda i, j: (i, j)),
        compiler_params=_cp(("parallel", "parallel"), VMEM_LIMIT),
        name="merge",
    )(x, bra, brb, w_out, w_out, proj, proj)


def _page_specs(block, n_pages, per_step):
    def spec(r):
        def index_map(b, p, pt):
            page = jnp.minimum(p * per_step + r, n_pages - 1)
            return (pt[b * n_pages + page],) + (0,) * (len(block) - 1)
        return pl.BlockSpec(block, index_map)
    return [spec(r) for r in range(per_step)]


def _new_rows(ref):
    n_new, width = ref.shape
    return jnp.concatenate([ref[...], jnp.zeros((PAGE - n_new, width), F32)], axis=0).astype(BF16)


def _kv_rows(refs, n_heads):
    def head(ref, g):
        return ref[pl.ds(g, PAGE, stride=n_heads), :]
    return jnp.concatenate(
        [jnp.concatenate([head(ref, g) for g in range(n_heads)], axis=1) for ref in refs], axis=0)


def _idx_sample_kernel(pt_ref, *refs, n_steps, n_new):
    kp_refs = refs[:IDX_PAGES]
    knew_ref, iq_ref, iw_ref, o_ref = refs[IDX_PAGES:]
    p = pl.program_id(1)

    def scores(keys):
        lg = lax.dot_general(iq_ref[...], keys, _NT, preferred_element_type=F32)
        acc = jnp.zeros((n_new, keys.shape[0]), F32)
        for h in range(IDX_HEADS):
            w = jnp.broadcast_to(iw_ref[:, h:h + 1], acc.shape)
            acc = acc + jnp.maximum(lg[h * n_new:(h + 1) * n_new, :], 0.0) * w
        return acc

    @pl.when(p < n_steps)
    def _():
        keys = jnp.concatenate([r[...] for r in kp_refs], axis=0).astype(BF16)
        o_ref[...] = scores(keys)

    @pl.when(p == n_steps)
    def _():
        acc = scores(_new_rows(knew_ref))
        t = lax.broadcasted_iota(I32, acc.shape, 0)
        c = lax.broadcasted_iota(I32, acc.shape, 1)
        o_ref[:, 0:PAGE] = jnp.where(c <= t, acc, -jnp.inf)
        o_ref[:, PAGE:] = jnp.full((n_new, o_ref.shape[1] - PAGE), -jnp.inf, F32)


def _idx_sample(pt_flat, cache_idx, knew_pad, iq_ht, iw, nseq, n_pages, n_new):
    n_steps = n_pages // IDX_PAGES
    width = IDX_PAGES * PAGE
    grid_spec = pltpu.PrefetchScalarGridSpec(
        num_scalar_prefetch=1,
        grid=(nseq, n_steps + 1),
        in_specs=_page_specs((None, PAGE, IDX_DIM), n_pages, IDX_PAGES) + [
            pl.BlockSpec((n_new, IDX_DIM), lambda b, p, pt: (b, 0)),
            pl.BlockSpec((None, IDX_HEADS * n_new, IDX_DIM), lambda b, p, pt: (b, 0, 0)),
            pl.BlockSpec((n_new, 128), lambda b, p, pt: (b, 0))],
        out_specs=pl.BlockSpec((n_new, width), lambda b, p, pt: (b, p)),
    )
    return pl.pallas_call(
        functools.partial(_idx_sample_kernel, n_steps=n_steps, n_new=n_new),
        out_shape=jax.ShapeDtypeStruct((nseq * n_new, (n_steps + 1) * width), F32),
        grid_spec=grid_spec,
        compiler_params=_cp(("arbitrary", "arbitrary")),
        name="idx_sample",
    )(pt_flat, *([cache_idx] * IDX_PAGES), knew_pad, iq_ht, iw)


def _thresh_kernel(s_ref, o_ref, *, n_past, n_new):
    rows, width = s_ref.shape
    s = s_ref[...]
    col = lax.broadcasted_iota(I32, (rows, width), 1)
    t_new = lax.broadcasted_iota(I32, (rows, width), 0) % n_new

    def count(mask):
        return jnp.sum(jnp.where(mask, 1.0, 0.0), axis=-1, keepdims=True)

    thr = _key_to_float(_kth_largest_key(lambda c: count(s >= c), (rows, 1)))
    need = IDX_TOPK - count(s > thr)
    tie_col = jnp.where(s == thr, col, 2 ** 14)

    def cut_body(b, cut):
        cand = cut + lax.shift_left(jnp.int32(1), 13 - b)
        return jnp.where(count(tie_col < cand) < need, cand, cut)

    cut = lax.fori_loop(0, 14, cut_body, jnp.zeros((rows, 1), I32))
    tie = jnp.where(s == thr, jnp.where(col <= cut, 0.0, NEG), NEG)
    picked = jnp.where(s > thr, 0.0, tie)
    o_ref[...] = jnp.where(col <= n_past + t_new, picked, NEG)


def _thresh(scores, n_past, n_new, rows):
    m, width = scores.shape
    return pl.pallas_call(
        functools.partial(_thresh_kernel, n_past=n_past, n_new=n_new),
        out_shape=jax.ShapeDtypeStruct((m, width), F32),
        grid=(m // rows,),
        in_specs=[pl.BlockSpec((rows, width), lambda i: (i, 0))],
        out_specs=pl.BlockSpec((rows, width), lambda i: (i, 0)),
        compiler_params=_cp(("parallel",), VMEM_LIMIT),
        name="thresh_sample",
    )(scores)


def _diag_blocks(x, n_groups):
    rows = x.shape[0] // n_groups
    return jnp.concatenate(
        [x[g * rows:(g + 1) * rows, g * 128:(g + 1) * 128] for g in range(n_groups)], axis=0)


def _dsa_sample_kernel(pt_ref, *refs, n_steps):
    kp_refs, vp_refs = refs[:KV_PAGES], refs[KV_PAGES:2 * KV_PAGES]
    (knew_ref, vnew_ref, q_ref, bias_ref, bias_new_ref, z_ref, o_ref,
     m_ref, l_ref, acc_ref) = refs[2 * KV_PAGES:]
    p = pl.program_id(1)

    @pl.when(p == 0)
    def _():
        m_ref[...] = jnp.full(m_ref.shape, NEG, F32)
        l_ref[...] = jnp.zeros(l_ref.shape, F32)
        acc_ref[...] = jnp.zeros(acc_ref.shape, F32)

    def step(k, v, bias8):
        s = lax.dot_general(q_ref[...], k, _NT, preferred_element_type=F32)
        s = s + _rep(bias8, 128 // bias8.shape[0], 0)
        m_prev = m_ref[...]
        m_new = jnp.maximum(m_prev, jnp.max(s, axis=-1, keepdims=True))
        alpha = jnp.exp(m_prev - m_new)
        pr = jnp.exp(s - _rep(m_new, s.shape[1] // 128, 1))
        l_ref[...] = alpha * l_ref[...] + jnp.sum(pr, axis=-1, keepdims=True)
        m_ref[...] = m_new
        pv = jnp.dot(pr.astype(BF16), v, preferred_element_type=F32)
        acc_ref[...] = acc_ref[...] * alpha + _diag_blocks(pv, A_KV)

    @pl.when(p < n_steps)
    def _():
        step(_kv_rows(kp_refs, A_KV).astype(BF16), _kv_rows(vp_refs, A_KV).astype(BF16), bias_ref[...])

    @pl.when(p == n_steps)
    def _():
        step(_new_rows(knew_ref), _new_rows(vnew_ref), bias_new_ref[...])
        o_ref[...] = (acc_ref[...] / l_ref[...] * _silu(z_ref[...])).astype(BF16)


def _dsa_sample(pt_flat, cache_k, cache_v, knew, vnew, qbd, bias, z, nseq, n_pages, n_new):
    n_steps = n_pages // KV_PAGES
    seq3 = lambda b, p, pt: (b, 0, 0)
    page_block = (None, PAGE * A_KV, HEAD_DIM)
    grid_spec = pltpu.PrefetchScalarGridSpec(
        num_scalar_prefetch=1,
        grid=(nseq, n_steps + 1),
        in_specs=_page_specs(page_block, n_pages, KV_PAGES) + _page_specs(page_block, n_pages, KV_PAGES) + [
            pl.BlockSpec((n_new, 1024), lambda b, p, pt: (b, 0)),
            pl.BlockSpec((n_new, 1024), lambda b, p, pt: (b, 0)),
            pl.BlockSpec((None, 128, 1024), seq3),
            pl.BlockSpec((n_new, KV_PAGES * PAGE), lambda b, p, pt: (b, jnp.minimum(p, n_steps - 1))),
            pl.BlockSpec((n_new, PAGE), lambda b, p, pt: (b, n_pages)),
            pl.BlockSpec((None, 128, 128), seq3)],
        out_specs=pl.BlockSpec((None, 128, 128), seq3),
        scratch_shapes=[pltpu.VMEM((128, 128), F32), pltpu.VMEM((128, 128), F32),
                        pltpu.VMEM((128, 128), F32)],
    )
    return pl.pallas_call(
        functools.partial(_dsa_sample_kernel, n_steps=n_steps),
        out_shape=jax.ShapeDtypeStruct((nseq, 128, 128), BF16),
        grid_spec=grid_spec,
        compiler_params=_cp(("arbitrary", "arbitrary"), VMEM_LIMIT),
        name="dsa_sample",
    )(pt_flat, *([cache_k] * KV_PAGES), *([cache_v] * KV_PAGES), knew, vnew, qbd, bias, bias, z)


def _moba_sample_kernel(pt_ref, *refs, n_steps, n_blocks, n_new):
    kp_refs, vp_refs = refs[:KV_PAGES], refs[KV_PAGES:2 * KV_PAGES]
    (knew_ref, vnew_ref, q_ref, z_ref, o_ref,
     m_ref, l_ref, acc_ref, km_ref) = refs[2 * KV_PAGES:]
    p = pl.program_id(1)
    ppb = MOBA_BLOCK // PAGE
    blocks_per_step = KV_PAGES // ppb

    def partial_softmax(k, v, bias):
        s = lax.dot_general(q_ref[...], k, _NT, preferred_element_type=F32)
        if bias is not None:
            s = s + bias
        m = jnp.max(s, axis=-1, keepdims=True)
        pr = jnp.exp(s - m)
        l = jnp.sum(pr, axis=-1, keepdims=True)
        pv = jnp.dot(pr.astype(BF16), v, preferred_element_type=F32)
        shape = (128, 128)
        return jnp.broadcast_to(m, shape), jnp.broadcast_to(l, shape), _diag_blocks(pv, B_KV)

    @pl.when(p == 0)
    def _():
        km_ref[...] = jnp.zeros(km_ref.shape, F32)

    @pl.when(p < n_steps)
    def _():
        for u in range(blocks_per_step):
            n = p * blocks_per_step + u
            k32 = _kv_rows(kp_refs[u * ppb:(u + 1) * ppb], B_KV)
            v = _kv_rows(vp_refs[u * ppb:(u + 1) * ppb], B_KV).astype(BF16)
            km_ref[pl.ds(n, 1), :] = jnp.sum(k32, axis=0, keepdims=True) * (1.0 / MOBA_BLOCK)
            m_ref[n], l_ref[n], acc_ref[n] = partial_softmax(k32.astype(BF16), v, None)

    @pl.when(p == n_steps)
    def _():
        gate = lax.dot_general(q_ref[...], km_ref[...].astype(BF16), _NT, preferred_element_type=F32)
        sel = _top_blocks(gate, n_blocks)
        lane = lax.broadcasted_iota(I32, (128, 128), 1)
        t = lax.broadcasted_iota(I32, (128, 128), 0) % n_new
        m_own, l_own, acc_own = partial_softmax(_new_rows(knew_ref), _new_rows(vnew_ref),
                                                jnp.where(lane <= t, 0.0, NEG))
        picked = [jnp.broadcast_to(sel[:, n:n + 1], (128, 128)) > 0.0 for n in range(n_blocks)]
        m_all = m_own
        for n in range(n_blocks):
            m_all = jnp.maximum(m_all, jnp.where(picked[n], m_ref[n], NEG))
        w = jnp.exp(m_own - m_all)
        num = w * acc_own
        den = w * l_own
        for n in range(n_blocks):
            w = jnp.where(picked[n], jnp.exp(m_ref[n] - m_all), 0.0)
            num = num + w * acc_ref[n]
            den = den + w * l_ref[n]
        o_ref[...] = (num / den * _silu(z_ref[...])).astype(BF16)


def _moba_sample(pt_flat, cache_k, cache_v, knew, vnew, qbd, z, nseq, n_pages, n_new):
    n_steps = n_pages // KV_PAGES
    n_blocks = n_pages * PAGE // MOBA_BLOCK
    seq3 = lambda b, p, pt: (b, 0, 0)
    page_block = (None, PAGE * B_KV, HEAD_DIM)
    grid_spec = pltpu.PrefetchScalarGridSpec(
        num_scalar_prefetch=1,
        grid=(nseq, n_steps + 1),
        in_specs=_page_specs(page_block, n_pages, KV_PAGES) + _page_specs(page_block, n_pages, KV_PAGES) + [
            pl.BlockSpec((n_new, 1024), lambda b, p, pt: (b, 0)),
            pl.BlockSpec((n_new, 1024), lambda b, p, pt: (b, 0)),
            pl.BlockSpec((None, 128, 1024), seq3),
            pl.BlockSpec((None, 128, 128), seq3)],
        out_specs=pl.BlockSpec((None, 128, 128), seq3),
        scratch_shapes=[pltpu.VMEM((n_blocks, 128, 128), F32), pltpu.VMEM((n_blocks, 128, 128), F32),
                        pltpu.VMEM((n_blocks, 128, 128), F32), pltpu.VMEM((128, 1024), F32)],
    )
    return pl.pallas_call(
        functools.partial(_moba_sample_kernel, n_steps=n_steps, n_blocks=n_blocks, n_new=n_new),
        out_shape=jax.ShapeDtypeStruct((nseq, 128, 128), BF16),
        grid_spec=grid_spec,
        compiler_params=_cp(("arbitrary", "arbitrary"), VMEM_LIMIT),
        name="moba_sample",
    )(pt_flat, *([cache_k] * KV_PAGES), *([cache_v] * KV_PAGES), knew, vnew, qbd, z)


def _rope_tables(pos):
    half = HEAD_DIM // 2
    inv_freq = ROPE_THETA ** (-jnp.arange(half, dtype=F32) / half)
    ang = pos.astype(F32)[:, None] * inv_freq[None, :]
    cos, sin = jnp.cos(ang), jnp.sin(ang)
    return jnp.concatenate([cos, cos], axis=-1), jnp.concatenate([-sin, sin], axis=-1)


IDX_COLS = 160
REPACK_ROWS = 64


def _pack_kernel(w_ref, main_ref, small_ref):
    main_ref[:, 0:OFF_AZ] = w_ref[:, 0:OFF_AZ].astype(BF16)
    main_ref[:, OFF_AZ:] = w_ref[:, OFF_AZ + IDX_COLS:].astype(BF16)
    small_ref[...] = w_ref[:, OFF_AZ:OFF_AZ + N_SMALL].astype(BF16)


def _pack_w_in(w):
    k, n_in = w.shape
    return pl.pallas_call(
        _pack_kernel,
        out_shape=(jax.ShapeDtypeStruct((k, N_MAIN), BF16), jax.ShapeDtypeStruct((k, N_SMALL), BF16)),
        grid=(k // REPACK_ROWS,),
        in_specs=[pl.BlockSpec((REPACK_ROWS, n_in), lambda i: (i, 0))],
        out_specs=(pl.BlockSpec((REPACK_ROWS, N_MAIN), lambda i: (i, 0)),
                   pl.BlockSpec((REPACK_ROWS, N_SMALL), lambda i: (i, 0))),
        compiler_params=_cp(("parallel",), VMEM_LIMIT),
        name="pack_w_in",
    )(w)


def _to_grt(u, nseq, n_new, n_groups):
    u = u.reshape(nseq, n_new, n_groups, 2, HEAD_DIM)
    return jnp.transpose(u, (0, 2, 3, 1, 4)).reshape(nseq, n_groups * 2 * n_new, HEAD_DIM)


def _from_grt(u, nseq, n_new, n_groups):
    u = u.reshape(nseq, n_groups, 2, n_new, HEAD_DIM)
    return jnp.transpose(u, (0, 3, 1, 2, 4)).reshape(nseq * n_new, n_groups * 2 * HEAD_DIM)


def _block_diag_q(q, nseq, n_new, n_groups):
    qg = _to_grt(q, nseq, n_new, n_groups).reshape(nseq, n_groups, 2 * n_new, 1, HEAD_DIM)
    eye = jnp.eye(n_groups, dtype=q.dtype).reshape(1, n_groups, 1, n_groups, 1)
    return (qg * eye).reshape(nseq, n_groups * 2 * n_new, n_groups * HEAD_DIM)


def kernel(x_prompt, x_sample, cache_a_k, cache_a_v, cache_idx_k, cache_b_k, cache_b_v, page_table,
           norm_gain, w_in, q_norm_a, k_norm_a, q_norm_b, k_norm_b, w_out):
    nb, seq, _ = x_prompt.shape
    nseq, n_new, _ = x_sample.shape
    n_pages = page_table.shape[1]
    past = n_pages * PAGE
    n_pool = cache_a_k.shape[1]
    assert norm_gain.shape[0] == 1 and seq % TK == 0 and 16 * n_new == 128 and nseq * n_new == TQ
    assert n_pages % IDX_PAGES == 0 and n_pages % KV_PAGES == 0
    assert n_pages * PAGE // MOBA_BLOCK <= 128 and seq // MOBA_BLOCK <= 128

    w_main, w_small = _pack_w_in(w_in[0])
    w_o = w_out[0].astype(BF16)
    gains = (q_norm_a, k_norm_a, q_norm_b, k_norm_b)

    xp = x_prompt.reshape(nb * seq, D_MODEL)
    xs = x_sample.reshape(nseq * n_new, D_MODEL)
    ms = xs.shape[0]

    mp = nb * seq
    h = _rmsnorm(xp, xs, norm_gain, ms)
    tm = (mp + ms) // 8
    proj = _matmul(h, w_main, tm, 1024, "proj")
    small = _matmul(h, w_small, tm, N_SMALL, "proj_small")

    cos_p, sin_p = _rope_tables(jnp.arange(seq))
    cos_s, sin_s = _rope_tables(past + jnp.arange(ms) % n_new)
    (qa_p, ka32_p, ka_p, va32_p, va_p, iq_p, ik32_p, ik_p, iw_p,
     qb_p, kb32_p, kb_p, vb32_p, vb_p, kmean_p) = _post(proj, small, cos_p, sin_p, *gains, mp, 0)
    (qa_s, ka32_s, _, va32_s, _, iq_s, ik32_s, _, iw_s,
     qb_s, kb32_s, _, vb32_s, _, _) = _post(proj, small, cos_s, sin_s, *gains, ms, mp)

    bra_p = _dsa_prompt(iq_p, iw_p, ik_p, qa_p, ka_p, va_p, proj, nb, seq)
    nblk = seq // MOBA_BLOCK
    km_p = jnp.pad(kmean_p.reshape(nb, nblk, 1024), ((0, 0), (0, 128 - nblk), (0, 0)))
    brb_p = _moba_prompt(qb_p, kb_p, vb_p, km_p, proj, nb, seq)
    y_p = _merge(xp, bra_p, brb_p, w_o, proj, 1024, 512, 0)

    pt_flat = page_table.reshape(-1).astype(I32)
    c_idx = cache_idx_k.reshape(n_pool, PAGE, IDX_DIM)
    pool = lambda c: c.reshape(n_pool, PAGE * c.shape[-2], HEAD_DIM)

    iq_ht = jnp.transpose(iq_s.reshape(nseq, n_new, IDX_HEADS, IDX_DIM), (0, 2, 1, 3))
    iq_ht = iq_ht.reshape(nseq, IDX_HEADS * n_new, IDX_DIM)
    scores = _idx_sample(pt_flat, c_idx, ik32_s, iq_ht, iw_s, nseq, n_pages, n_new)
    bias_s = _thresh(scores, past, n_new, 64)
    za = _to_grt(proj[mp:, OFF_AZ:OFF_AZ + BRANCH_W], nseq, n_new, A_KV)
    bra_s = _dsa_sample(pt_flat, pool(cache_a_k), pool(cache_a_v), ka32_s, va32_s,
                        _block_diag_q(qa_s, nseq, n_new, A_KV), bias_s, za, nseq, n_pages, n_new)
    bra_s = _from_grt(bra_s, nseq, n_new, A_KV)

    zb = _to_grt(proj[mp:, OFF_BZ:OFF_BZ + BRANCH_W], nseq, n_new, B_KV)
    brb_s = _moba_sample(pt_flat, pool(cache_b_k), pool(cache_b_v), kb32_s, vb32_s,
                         _block_diag_q(qb_s, nseq, n_new, B_KV), zb, nseq, n_pages, n_new)
    brb_s = _from_grt(brb_s, nseq, n_new, B_KV)
    y_s = _merge(xs, bra_s, brb_s, w_o, proj, ms, 512, mp)

    kv = lambda u, b, t: u.reshape(1, b, t, 8, HEAD_DIM)
    ix = lambda u, b, t: u.reshape(1, b, t, IDX_DIM)
    return (y_p.reshape(nb, seq, D_MODEL), y_s.reshape(nseq, n_new, D_MODEL),
            kv(ka32_p, nb, seq), kv(va32_p, nb, seq), ix(ik32_p, nb, seq),
            kv(kb32_p, nb, seq), kv(vb32_p, nb, seq),
            kv(ka32_s, nseq, n_new), kv(va32_s, nseq, n_new), ix(ik32_s, nseq, n_new),
            kv(kb32_s, nseq, n_new), kv(vb32_s, nseq, n_new))
```

```python
import functools

import jax
import jax.numpy as jnp
from jax import lax
from jax.experimental import pallas as pl
from jax.experimental.pallas import tpu as pltpu

F32 = jnp.float32
BF16 = jnp.bfloat16
I32 = jnp.int32

D_MODEL = 4096
HEAD_DIM = 128
A_HEADS = 16
A_KV = 8
IDX_HEADS = 32
IDX_DIM = 128
IDX_TOPK = 256
B_HEADS = 16
B_KV = 8
MOBA_BLOCK = 256
MOBA_TOPK = 3
BRANCH_W = 2048
ROPE_THETA = 10000.0
EPS = 1e-6
PAGE = 128

OFF_AQ, OFF_AK, OFF_AV, OFF_IQ, OFF_AZ = 0, 2048, 3072, 4096, 8192
OFF_BQ, OFF_BK, OFF_BV, OFF_BZ, OFF_GATE = 10240, 12288, 13312, 14336, 16384
N_MAIN = 24576
N_SMALL = 256

NEG = -1e30
INT_MIN = -(2 ** 31)
KEY_NEG_INF = -2139095041
TQ = 256
TK = 512
IDX_PAGES = 16
KV_PAGES = 8
VMEM_LIMIT = 56 * 1024 * 1024

_NT = (((1,), (1,)), ((), ()))


def _cp(sem, vmem=None):
    return pltpu.CompilerParams(dimension_semantics=sem, vmem_limit_bytes=vmem)


def _rmsnorm_kernel(x_ref, g_ref, o_ref):
    x = x_ref[...]
    ms = jnp.mean(x * x, axis=-1, keepdims=True)
    o_ref[...] = (x * lax.rsqrt(ms + EPS) * g_ref[...]).astype(BF16)


def _rmsnorm2_kernel(xa_ref, xb_ref, g_ref, o_ref, *, na):
    i = pl.program_id(0)

    @pl.when(i < na)
    def _():
        _rmsnorm_kernel(xa_ref, g_ref, o_ref)

    @pl.when(i >= na)
    def _():
        _rmsnorm_kernel(xb_ref, g_ref, o_ref)


def _rmsnorm(xa, xb, gain, tm):
    na, nb = xa.shape[0] // tm, xb.shape[0] // tm
    return pl.pallas_call(
        functools.partial(_rmsnorm2_kernel, na=na),
        out_shape=jax.ShapeDtypeStruct(((na + nb) * tm, D_MODEL), BF16),
        grid=(na + nb,),
        in_specs=[pl.BlockSpec((tm, D_MODEL), lambda i: (jnp.minimum(i, na - 1), 0)),
                  pl.BlockSpec((tm, D_MODEL), lambda i: (jnp.maximum(i - na, 0), 0)),
                  pl.BlockSpec((1, D_MODEL), lambda i: (0, 0))],
        out_specs=pl.BlockSpec((tm, D_MODEL), lambda i: (i, 0)),
        compiler_params=_cp(("arbitrary",)),
        name="rmsnorm",
    )(xa, xb, gain)


def _matmul_kernel(x_ref, w_ref, o_ref):
    o_ref[...] = jnp.dot(x_ref[...], w_ref[...], preferred_element_type=F32)


def _matmul(x, w, tm, tn, name):
    m, k = x.shape
    n = w.shape[1]
    return pl.pallas_call(
        _matmul_kernel,
        out_shape=jax.ShapeDtypeStruct((m, n), F32),
        grid=(m // tm, n // tn),
        in_specs=[pl.BlockSpec((tm, k), lambda i, j: (i, 0)),
                  pl.BlockSpec((k, tn), lambda i, j: (0, j))],
        out_specs=pl.BlockSpec((tm, tn), lambda i, j: (i, j)),
        compiler_params=_cp(("parallel", "parallel"), VMEM_LIMIT),
        name=name,
    )(x, w)


def _rope(y, cos, sin):
    return y * cos + pltpu.roll(y, 64, 1) * sin


def _head_norm(x, gain):
    ms = jnp.mean(x * x, axis=-1, keepdims=True)
    return x * lax.rsqrt(ms + EPS) * gain


def _post_kernel(pa_ref, piq_ref, pbq_ref, pbkv_ref, ps_ref, cos_ref, sin_ref,
                 qna_ref, kna_ref, qnb_ref, knb_ref,
                 qa_ref, ka32_ref, ka_ref, va32_ref, va_ref, iq_ref, ik32_ref, ik_ref, iw_ref,
                 qb_ref, kb32_ref, kb_ref, vb32_ref, vb_ref, kmean_ref):
    cos = cos_ref[...]
    sin = sin_ref[...]
    scale = HEAD_DIM ** -0.5
    hs = lambda h: slice(h * HEAD_DIM, (h + 1) * HEAD_DIM)

    qna, kna, qnb, knb = qna_ref[...], kna_ref[...], qnb_ref[...], knb_ref[...]
    for h in range(A_HEADS):
        q = _rope(_head_norm(pa_ref[:, hs(h)], qna), cos, sin)
        qa_ref[:, hs(h)] = (q * scale).astype(BF16)
    for h in range(A_KV):
        k = _rope(_head_norm(pa_ref[:, hs(A_HEADS + h)], kna), cos, sin)
        ka32_ref[:, hs(h)] = k
        ka_ref[:, hs(h)] = k.astype(BF16)
        v = pa_ref[:, hs(A_HEADS + A_KV + h)]
        va32_ref[:, hs(h)] = v
        va_ref[:, hs(h)] = v.astype(BF16)
    for h in range(IDX_HEADS):
        iq_ref[:, hs(h)] = _rope(piq_ref[:, hs(h)], cos, sin).astype(BF16)
    ik = _rope(ps_ref[:, 0:IDX_DIM], cos, sin)
    ik32_ref[...] = ik
    ik_ref[...] = ik.astype(BF16)
    iw_ref[...] = ps_ref[:, IDX_DIM:2 * IDX_DIM] * (IDX_HEADS ** -0.5 * IDX_DIM ** -0.5)
    for h in range(B_HEADS):
        q = _rope(_head_norm(pbq_ref[:, hs(h)], qnb), cos, sin)
        qb_ref[:, hs(h)] = (q * scale).astype(BF16)
    for h in range(B_KV):
        k = _rope(_head_norm(pbkv_ref[:, hs(h)], knb), cos, sin)
        kb32_ref[:, hs(h)] = k
        kb_ref[:, hs(h)] = k.astype(BF16)
        kmean_ref[0, :, hs(h)] = jnp.mean(k, axis=0, keepdims=True)
        v = pbkv_ref[:, hs(B_KV + h)]
        vb32_ref[:, hs(h)] = v
        vb_ref[:, hs(h)] = v.astype(BF16)


def _post(proj, small, cos, sin, qna, kna, qnb, knb, m, row0):
    r = TQ
    nt = cos.shape[0] // r
    t0 = row0 // r
    row = lambda c: (lambda i: (i, c))
    src = lambda c: (lambda i: (t0 + i, c))
    gain = pl.BlockSpec((1, HEAD_DIM), lambda i: (0, 0))
    tab = pl.BlockSpec((r, HEAD_DIM), lambda i: (i % nt, 0))
    sds = jax.ShapeDtypeStruct
    kv32 = sds((m, 1024), F32)
    kvbf = sds((m, 1024), BF16)
    kvspec = pl.BlockSpec((r, 1024), row(0))
    out_shape = (sds((m, 2048), BF16), kv32, kvbf, kv32, kvbf,
                 sds((m, 4096), BF16), sds((m, 128), F32), sds((m, 128), BF16), sds((m, 128), F32),
                 sds((m, 2048), BF16), kv32, kvbf, kv32, kvbf,
                 sds((m // r, 1, 1024), F32))
    s128 = pl.BlockSpec((r, 128), row(0))
    out_specs = (pl.BlockSpec((r, 2048), row(0)), kvspec, kvspec, kvspec, kvspec,
                 pl.BlockSpec((r, 4096), row(0)), s128, s128, s128,
                 pl.BlockSpec((r, 2048), row(0)), kvspec, kvspec, kvspec, kvspec,
                 pl.BlockSpec((1, 1, 1024), lambda i: (i, 0, 0)))
    return pl.pallas_call(
        _post_kernel,
        out_shape=out_shape,
        grid=(m // r,),
        in_specs=[pl.BlockSpec((r, 4096), src(0)),
                  pl.BlockSpec((r, 4096), src(1)),
                  pl.BlockSpec((r, 2048), src(OFF_BQ // 2048)),
                  pl.BlockSpec((r, 2048), src(OFF_BK // 2048)),
                  pl.BlockSpec((r, N_SMALL), src(0)),
                  tab, tab, gain, gain, gain, gain],
        out_specs=out_specs,
        compiler_params=_cp(("parallel",), VMEM_LIMIT),
        name="post",
    )(proj, proj, proj, proj, small, cos, sin, qna, kna, qnb, knb)


def _key_to_float(key):
    bits = jnp.where(key >= 0, key, key ^ jnp.int32(0x7FFFFFFF))
    return lax.bitcast_convert_type(bits, F32)


def _kth_largest_key(count_ge, shape):
    def bit_body(b, thr):
        cand = thr + lax.shift_left(jnp.int32(1), 31 - b)
        n = count_ge(_key_to_float(cand))
        return jnp.where(cand <= KEY_NEG_INF, cand, jnp.where(n >= IDX_TOPK, cand, thr))

    return lax.fori_loop(0, 32, bit_body, jnp.full(shape, INT_MIN, I32))


def _silu(z):
    return z * jax.nn.sigmoid(z)


def _rep(x, n, axis):
    return jnp.concatenate([x] * n, axis=axis)


ROW_BLOCK = 64


def _attend_init(m_ref, l_ref, acc_ref):
    m_ref[...] = jnp.full(m_ref.shape, NEG, F32)
    l_ref[...] = jnp.zeros(l_ref.shape, F32)
    acc_ref[...] = jnp.zeros(acc_ref.shape, F32)


def _attend_tile(q2, k, v, bias_rows, s_ref, p_ref, m_ref, l_ref, acc_ref):
    s_ref[...] = lax.dot_general(q2, k, _NT, preferred_element_type=F32)
    reps = k.shape[0] // 128
    for b in range(q2.shape[0] // ROW_BLOCK):
        rs = slice(b * ROW_BLOCK, (b + 1) * ROW_BLOCK)
        s = s_ref[rs, :] + bias_rows(b)
        m_prev = m_ref[rs, :]
        m_new = jnp.maximum(m_prev, jnp.max(s, axis=-1, keepdims=True))
        alpha = jnp.exp(m_prev - m_new)
        p = jnp.exp(s - _rep(m_new, reps, 1))
        l_ref[rs, :] = alpha * l_ref[rs, :] + jnp.sum(p, axis=-1, keepdims=True)
        m_ref[rs, :] = m_new
        acc_ref[rs, :] = acc_ref[rs, :] * alpha
        p_ref[rs, :] = p.astype(BF16)
    acc_ref[...] += jnp.dot(p_ref[...], v, preferred_element_type=F32)


def _dsa_prompt_kernel(iq_ref, iw_ref, ik_ref, qa_ref, ka_ref, va_ref, az_ref, o_ref,
                       wb_ref, sc_ref, bias_ref, cut_ref, s_ref, p_ref, m_ref, l_ref, acc_ref):
    i = pl.program_id(1)
    g = pl.program_id(2)
    n_tiles = i + 1
    half = TQ // 2
    lane = lax.broadcasted_iota(I32, (TQ, 128), 1)
    rowg = lax.broadcasted_iota(I32, (TQ, 128), 0) + i * TQ

    @pl.when(g == 0)
    def _select():
        for h in range(IDX_HEADS):
            wb_ref[h] = jnp.broadcast_to(iw_ref[:, h:h + 1], (TQ, 128))

        def score_tile(j, _):
            ks = ik_ref[pl.ds(pl.multiple_of(j * TQ, TQ), TQ), :]
            for rh in range(2):
                rs = slice(rh * half, (rh + 1) * half)
                acc0 = jnp.zeros((half, 128), F32)
                acc1 = jnp.zeros((half, 128), F32)
                for h in range(IDX_HEADS):
                    lg = lax.dot_general(iq_ref[rs, h * 128:(h + 1) * 128], ks, _NT,
                                         preferred_element_type=F32)
                    lg = jnp.maximum(lg, 0.0)
                    w = wb_ref[h, rs, :]
                    acc0 = acc0 + lg[:, :128] * w
                    acc1 = acc1 + lg[:, 128:] * w
                colg = lax.broadcasted_iota(I32, (half, 128), 1) + j * TQ
                rg = lax.broadcasted_iota(I32, (half, 128), 0) + (i * TQ + rh * half)
                sc_ref[j, rs, 0:128] = jnp.where(colg <= rg, acc0, -jnp.inf)
                sc_ref[j, rs, 128:256] = jnp.where(colg + 128 <= rg, acc1, -jnp.inf)
            return 0

        lax.fori_loop(0, n_tiles, score_tile, 0)

        def count(pred):
            def body(j, cnt):
                t = sc_ref[j]
                colg = lane + j * TQ
                return (cnt + jnp.where(pred(t[:, :128], colg), 1.0, 0.0)
                        + jnp.where(pred(t[:, 128:], colg + 128), 1.0, 0.0))
            cnt = lax.fori_loop(0, n_tiles, body, jnp.zeros((TQ, 128), F32))
            return jnp.sum(cnt, axis=-1, keepdims=True)

        thr = _key_to_float(_kth_largest_key(lambda c: count(lambda t, _: t >= c), (TQ, 128)))
        need = IDX_TOPK - count(lambda t, _: t > thr)
        n_tie = count(lambda t, _: t == thr)

        cut_ref[...] = jnp.full((TQ, 128), 4096, I32)

        @pl.when(jnp.max(n_tie - need) > 0.0)
        def _ties():
            def cut_body(b, cut):
                cand = cut + lax.shift_left(jnp.int32(1), 11 - b)
                n = count(lambda t, c: jnp.where(t == thr, c, 4096) < cand)
                return jnp.where(n < need, cand, cut)
            cut_ref[...] = lax.fori_loop(0, 12, cut_body, jnp.zeros((TQ, 128), I32))

        cut = cut_ref[...]

        def bias_tile(j, _):
            t = sc_ref[j]
            for hh in range(2):
                th = t[:, hh * 128:(hh + 1) * 128]
                colg = lane + (j * TQ + hh * 128)
                tie = jnp.where(th == thr, jnp.where(colg <= cut, 0.0, NEG), NEG)
                picked = jnp.where(th > thr, 0.0, tie)
                bias_ref[j, :, hh * 128:(hh + 1) * 128] = jnp.where(colg <= rowg, picked, NEG)
            return 0

        lax.fori_loop(0, n_tiles, bias_tile, 0)

        @pl.when(i % 2 == 0)
        def _pad_tile():
            bias_ref[i + 1] = jnp.full((TQ, TQ), NEG, F32)

    q2 = jnp.concatenate([qa_ref[:, 0:128], qa_ref[:, 128:256]], axis=0)
    _attend_init(m_ref, l_ref, acc_ref)

    def body(jj, _):
        off = pl.multiple_of(jj * TK, TK)

        def bias_rows(b):
            rs = slice(b * ROW_BLOCK % TQ, b * ROW_BLOCK % TQ + ROW_BLOCK)
            return jnp.concatenate([bias_ref[2 * jj, rs, :], bias_ref[2 * jj + 1, rs, :]], axis=1)

        _attend_tile(q2, ka_ref[pl.ds(off, TK), :], va_ref[pl.ds(off, TK), :], bias_rows,
                     s_ref, p_ref, m_ref, l_ref, acc_ref)
        return 0

    lax.fori_loop(0, (i + 2) // 2, body, 0)
    for r in range(2):
        cs = slice(r * 128, (r + 1) * 128)
        rs = slice(r * TQ, (r + 1) * TQ)
        o_ref[:, cs] = (acc_ref[rs, :] / l_ref[rs, :] * _silu(az_ref[:, cs])).astype(BF16)


def _dsa_prompt(iq, iw, ik, qa, ka, va, proj, nb, seq):
    nq = seq // TQ
    m = nb * seq
    rowt = lambda b, i, g: b * nq + i
    return pl.pallas_call(
        _dsa_prompt_kernel,
        out_shape=jax.ShapeDtypeStruct((m, BRANCH_W), BF16),
        grid=(nb, nq, A_KV),
        in_specs=[pl.BlockSpec((TQ, 4096), lambda b, i, g: (rowt(b, i, g), 0)),
                  pl.BlockSpec((TQ, 128), lambda b, i, g: (rowt(b, i, g), 0)),
                  pl.BlockSpec((seq, 128), lambda b, i, g: (b, 0)),
                  pl.BlockSpec((TQ, 256), lambda b, i, g: (rowt(b, i, g), g)),
                  pl.BlockSpec((seq, 128), lambda b, i, g: (b, g)),
                  pl.BlockSpec((seq, 128), lambda b, i, g: (b, g)),
                  pl.BlockSpec((TQ, 256), lambda b, i, g: (rowt(b, i, g), OFF_AZ // 256 + g))],
        out_specs=pl.BlockSpec((TQ, 256), lambda b, i, g: (rowt(b, i, g), g)),
        scratch_shapes=[pltpu.VMEM((IDX_HEADS, TQ, 128), F32),
                        pltpu.VMEM((seq // TQ, TQ, TQ), F32),
                        pltpu.VMEM((seq // TQ, TQ, TQ), F32),
                        pltpu.VMEM((TQ, 128), I32),
                        pltpu.VMEM((2 * TQ, TK), F32), pltpu.VMEM((2 * TQ, TK), BF16),
                        pltpu.VMEM((2 * TQ, 128), F32), pltpu.VMEM((2 * TQ, 128), F32),
                        pltpu.VMEM((2 * TQ, 128), F32)],
        compiler_params=_cp(("arbitrary", "arbitrary", "arbitrary"), VMEM_LIMIT),
        name="dsa_prompt",
    )(iq, iw, ik, qa, ka, va, proj)


def _top_blocks(gate, n_valid):
    lane = lax.broadcasted_iota(I32, gate.shape, 1).astype(F32)
    valid = lane < n_valid
    rem = jnp.where(valid, gate, -jnp.inf)
    sel = jnp.zeros(gate.shape, F32)
    for _ in range(MOBA_TOPK):
        mx = jnp.max(rem, axis=-1, keepdims=True)
        first = jnp.min(jnp.where(rem == mx, lane, 128.0), axis=-1, keepdims=True)
        pick = lane == first
        sel = jnp.where(pick, 1.0, sel)
        rem = jnp.where(pick, -jnp.inf, rem)
    return jnp.where(valid, sel, 0.0)


def _block_bias(sel, lane, n):
    picked = jnp.max(jnp.where(lane == n, sel, 0.0), axis=-1, keepdims=True)
    return jnp.where(picked > 0.0, 0.0, NEG)


def _moba_prompt_kernel(qb_ref, kb_ref, vb_ref, km_ref, bz_ref, o_ref, s_ref, p_ref, m_ref, l_ref, acc_ref):
    i = pl.program_id(1)
    f = i // 2
    lane = lax.broadcasted_iota(I32, (TQ, 128), 1)
    row = lax.broadcasted_iota(I32, (TQ, TQ), 0)
    col = lax.broadcasted_iota(I32, (TQ, TQ), 1)
    own_bias = jnp.where(col <= row, 0.0, NEG)
    neg_tile = jnp.full((TQ, TQ), NEG, F32)
    km = km_ref[...].astype(BF16)

    q2 = jnp.concatenate([qb_ref[:, 0:128], qb_ref[:, 128:256]], axis=0)
    gate = lax.dot_general(q2, km, _NT, preferred_element_type=F32)
    sels = [_top_blocks(gate[r * TQ:(r + 1) * TQ], i) for r in range(2)]
    _attend_init(m_ref, l_ref, acc_ref)
    per_head = TQ // ROW_BLOCK

    def kv(jj):
        off = pl.multiple_of(jj * TK, TK)
        return kb_ref[pl.ds(off, TK), :], vb_ref[pl.ds(off, TK), :]

    def body(jj, _):
        cols = [[_block_bias(sels[r], lane, 2 * jj + u) for u in range(2)] for r in range(2)]

        def bias_rows(b):
            r, rs = b // per_head, slice(b % per_head * ROW_BLOCK, (b % per_head + 1) * ROW_BLOCK)
            return jnp.concatenate([jnp.broadcast_to(c[rs], (ROW_BLOCK, TQ)) for c in cols[r]], axis=1)

        _attend_tile(q2, *kv(jj), bias_rows, s_ref, p_ref, m_ref, l_ref, acc_ref)
        return 0

    lax.fori_loop(0, f, body, 0)

    even = i % 2 == 0
    last = []
    for r in range(2):
        past = jnp.broadcast_to(_block_bias(sels[r], lane, 2 * f), (TQ, TQ))
        last.append(jnp.concatenate([jnp.where(even, own_bias, past),
                                     jnp.where(even, neg_tile, own_bias)], axis=1))

    def last_rows(b):
        r = b // per_head
        return last[r][b % per_head * ROW_BLOCK:(b % per_head + 1) * ROW_BLOCK]

    _attend_tile(q2, *kv(f), last_rows, s_ref, p_ref, m_ref, l_ref, acc_ref)
    for r in range(2):
        cs = slice(r * 128, (r + 1) * 128)
        rs = slice(r * TQ, (r + 1) * TQ)
        o_ref[:, cs] = (acc_ref[rs, :] / l_ref[rs, :] * _silu(bz_ref[:, cs])).astype(BF16)


def _moba_prompt(qb, kb, vb, kmean, proj, nb, seq):
    nq = seq // TQ
    m = nb * seq
    rowt = lambda b, i, g: b * nq + i
    return pl.pallas_call(
        _moba_prompt_kernel,
        out_shape=jax.ShapeDtypeStruct((m, BRANCH_W), BF16),
        grid=(nb, nq, B_KV),
        in_specs=[pl.BlockSpec((TQ, 256), lambda b, i, g: (rowt(b, i, g), g)),
                  pl.BlockSpec((seq, 128), lambda b, i, g: (b, g)),
                  pl.BlockSpec((seq, 128), lambda b, i, g: (b, g)),
                  pl.BlockSpec((None, 128, 128), lambda b, i, g: (b, 0, g)),
                  pl.BlockSpec((TQ, 256), lambda b, i, g: (rowt(b, i, g), OFF_BZ // 256 + g))],
        out_specs=pl.BlockSpec((TQ, 256), lambda b, i, g: (rowt(b, i, g), g)),
        scratch_shapes=[pltpu.VMEM((2 * TQ, TK), F32), pltpu.VMEM((2 * TQ, TK), BF16),
                        pltpu.VMEM((2 * TQ, 128), F32), pltpu.VMEM((2 * TQ, 128), F32),
                        pltpu.VMEM((2 * TQ, 128), F32)],
        compiler_params=_cp(("arbitrary", "arbitrary", "arbitrary"), VMEM_LIMIT),
        name="moba_prompt",
    )(qb, kb, vb, kmean, proj)


def _merge_kernel(x_ref, a_ref, b_ref, wa_ref, wb_ref, ga_ref, gb_ref, o_ref):
    pa = jnp.dot(a_ref[...], wa_ref[...], preferred_element_type=F32)
    pb = jnp.dot(b_ref[...], wb_ref[...], preferred_element_type=F32)
    o_ref[...] = x_ref[...] + (jax.nn.sigmoid(ga_ref[...]) * pa + jax.nn.sigmoid(gb_ref[...]) * pb)


def _merge(x, bra, brb, w_out, proj, tm, tn, row0):
    m = x.shape[0]
    t0 = row0 // tm
    ga0 = OFF_GATE // tn
    gb0 = (OFF_GATE + D_MODEL) // tn
    return pl.pallas_call(
        _merge_kernel,
        out_shape=jax.ShapeDtypeStruct((m, D_MODEL), F32),
        grid=(m // tm, D_MODEL // tn),
        in_specs=[pl.BlockSpec((tm, tn), lambda i, j: (i, j)),
                  pl.BlockSpec((tm, BRANCH_W), lambda i, j: (i, 0)),
                  pl.BlockSpec((tm, BRANCH_W), lambda i, j: (i, 0)),
                  pl.BlockSpec((None, BRANCH_W, tn), lambda i, j: (0, 0, j)),
                  pl.BlockSpec((None, BRANCH_W, tn), lambda i, j: (1, 0, j)),
                  pl.BlockSpec((tm, tn), lambda i, j: (t0 + i, ga0 + j)),
                  pl.BlockSpec((tm, tn), lambda i, j: (t0 + i, gb0 + j))],
        out_specs=pl.BlockSpec((tm, tn), lambda i, j: (i, j)),
        compiler_params=_cp(("parallel", "parallel"), VMEM_LIMIT),
        name="merge",
    )(x, bra, brb, w_out, w_out, proj, proj)


def _page_specs(block, n_pages, per_step):
    def spec(r):
        def index_map(b, p, pt):
            page = jnp.minimum(p * per_step + r, n_pages - 1)
            return (pt[b * n_pages + page],) + (0,) * (len(block) - 1)
        return pl.BlockSpec(block, index_map)
    return [spec(r) for r in range(per_step)]


def _new_rows(ref):
    n_new, width = ref.shape
    return jnp.concatenate([ref[...], jnp.zeros((PAGE - n_new, width), F32)], axis=0).astype(BF16)


def _kv_rows(refs, n_heads):
    def head(ref, g):
        return ref[pl.ds(g, PAGE, stride=n_heads), :]
    return jnp.concatenate(
        [jnp.concatenate([head(ref, g) for g in range(n_heads)], axis=1) for ref in refs], axis=0)


def _idx_sample_kernel(pt_ref, *refs, n_steps, n_new):
    kp_refs = refs[:IDX_PAGES]
    knew_ref, iq_ref, iw_ref, o_ref = refs[IDX_PAGES:]
    p = pl.program_id(1)

    def scores(keys):
        lg = lax.dot_general(iq_ref[...], keys, _NT, preferred_element_type=F32)
        acc = jnp.zeros((n_new, keys.shape[0]), F32)
        for h in range(IDX_HEADS):
            w = jnp.broadcast_to(iw_ref[:, h:h + 1], acc.shape)
            acc = acc + jnp.maximum(lg[h * n_new:(h + 1) * n_new, :], 0.0) * w
        return acc

    @pl.when(p < n_steps)
    def _():
        keys = jnp.concatenate([r[...] for r in kp_refs], axis=0).astype(BF16)
        o_ref[...] = scores(keys)

    @pl.when(p == n_steps)
    def _():
        acc = scores(_new_rows(knew_ref))
        t = lax.broadcasted_iota(I32, acc.shape, 0)
        c = lax.broadcasted_iota(I32, acc.shape, 1)
        o_ref[:, 0:PAGE] = jnp.where(c <= t, acc, -jnp.inf)
        o_ref[:, PAGE:] = jnp.full((n_new, o_ref.shape[1] - PAGE), -jnp.inf, F32)


def _idx_sample(pt_flat, cache_idx, knew_pad, iq_ht, iw, nseq, n_pages, n_new):
    n_steps = n_pages // IDX_PAGES
    width = IDX_PAGES * PAGE
    grid_spec = pltpu.PrefetchScalarGridSpec(
        num_scalar_prefetch=1,
        grid=(nseq, n_steps + 1),
        in_specs=_page_specs((None, PAGE, IDX_DIM), n_pages, IDX_PAGES) + [
            pl.BlockSpec((n_new, IDX_DIM), lambda b, p, pt: (b, 0)),
            pl.BlockSpec((None, IDX_HEADS * n_new, IDX_DIM), lambda b, p, pt: (b, 0, 0)),
            pl.BlockSpec((n_new, 128), lambda b, p, pt: (b, 0))],
        out_specs=pl.BlockSpec((n_new, width), lambda b, p, pt: (b, p)),
    )
    return pl.pallas_call(
        functools.partial(_idx_sample_kernel, n_steps=n_steps, n_new=n_new),
        out_shape=jax.ShapeDtypeStruct((nseq * n_new, (n_steps + 1) * width), F32),
        grid_spec=grid_spec,
        compiler_params=_cp(("arbitrary", "arbitrary")),
        name="idx_sample",
    )(pt_flat, *([cache_idx] * IDX_PAGES), knew_pad, iq_ht, iw)


def _thresh_kernel(s_ref, o_ref, *, n_past, n_new):
    rows, width = s_ref.shape
    s = s_ref[...]
    col = lax.broadcasted_iota(I32, (rows, width), 1)
    t_new = lax.broadcasted_iota(I32, (rows, width), 0) % n_new

    def count(mask):
        return jnp.sum(jnp.where(mask, 1.0, 0.0), axis=-1, keepdims=True)

    thr = _key_to_float(_kth_largest_key(lambda c: count(s >= c), (rows, 1)))
    need = IDX_TOPK - count(s > thr)
    tie_col = jnp.where(s == thr, col, 2 ** 14)

    def cut_body(b, cut):
        cand = cut + lax.shift_left(jnp.int32(1), 13 - b)
        return jnp.where(count(tie_col < cand) < need, cand, cut)

    cut = lax.fori_loop(0, 14, cut_body, jnp.zeros((rows, 1), I32))
    tie = jnp.where(s == thr, jnp.where(col <= cut, 0.0, NEG), NEG)
    picked = jnp.where(s > thr, 0.0, tie)
    o_ref[...] = jnp.where(col <= n_past + t_new, picked, NEG)


def _thresh(scores, n_past, n_new, rows):
    m, width = scores.shape
    return pl.pallas_call(
        functools.partial(_thresh_kernel, n_past=n_past, n_new=n_new),
        out_shape=jax.ShapeDtypeStruct((m, width), F32),
        grid=(m // rows,),
        in_specs=[pl.BlockSpec((rows, width), lambda i: (i, 0))],
        out_specs=pl.BlockSpec((rows, width), lambda i: (i, 0)),
        compiler_params=_cp(("parallel",), VMEM_LIMIT),
        name="thresh_sample",
    )(scores)


def _diag_blocks(x, n_groups):
    rows = x.shape[0] // n_groups
    return jnp.concatenate(
        [x[g * rows:(g + 1) * rows, g * 128:(g + 1) * 128] for g in range(n_groups)], axis=0)


def _dsa_sample_kernel(pt_ref, *refs, n_steps):
    kp_refs, vp_refs = refs[:KV_PAGES], refs[KV_PAGES:2 * KV_PAGES]
    (knew_ref, vnew_ref, q_ref, bias_ref, bias_new_ref, z_ref, o_ref,
     m_ref, l_ref, acc_ref) = refs[2 * KV_PAGES:]
    p = pl.program_id(1)

    @pl.when(p == 0)
    def _():
        m_ref[...] = jnp.full(m_ref.shape, NEG, F32)
        l_ref[...] = jnp.zeros(l_ref.shape, F32)
        acc_ref[...] = jnp.zeros(acc_ref.shape, F32)

    def step(k, v, bias8):
        s = lax.dot_general(q_ref[...], k, _NT, preferred_element_type=F32)
        s = s + _rep(bias8, 128 // bias8.shape[0], 0)
        m_prev = m_ref[...]
        m_new = jnp.maximum(m_prev, jnp.max(s, axis=-1, keepdims=True))
        alpha = jnp.exp(m_prev - m_new)
        pr = jnp.exp(s - _rep(m_new, s.shape[1] // 128, 1))
        l_ref[...] = alpha * l_ref[...] + jnp.sum(pr, axis=-1, keepdims=True)
        m_ref[...] = m_new
        pv = jnp.dot(pr.astype(BF16), v, preferred_element_type=F32)
        acc_ref[...] = acc_ref[...] * alpha + _diag_blocks(pv, A_KV)

    @pl.when(p < n_steps)
    def _():
        step(_kv_rows(kp_refs, A_KV).astype(BF16), _kv_rows(vp_refs, A_KV).astype(BF16), bias_ref[...])

    @pl.when(p == n_steps)
    def _():
        step(_new_rows(knew_ref), _new_rows(vnew_ref), bias_new_ref[...])
        o_ref[...] = (acc_ref[...] / l_ref[...] * _silu(z_ref[...])).astype(BF16)


def _dsa_sample(pt_flat, cache_k, cache_v, knew, vnew, qbd, bias, z, nseq, n_pages, n_new):
    n_steps = n_pages // KV_PAGES
    seq3 = lambda b, p, pt: (b, 0, 0)
    page_block = (None, PAGE * A_KV, HEAD_DIM)
    grid_spec = pltpu.PrefetchScalarGridSpec(
        num_scalar_prefetch=1,
        grid=(nseq, n_steps + 1),
        in_specs=_page_specs(page_block, n_pages, KV_PAGES) + _page_specs(page_block, n_pages, KV_PAGES) + [
            pl.BlockSpec((n_new, 1024), lambda b, p, pt: (b, 0)),
            pl.BlockSpec((n_new, 1024), lambda b, p, pt: (b, 0)),
            pl.BlockSpec((None, 128, 1024), seq3),
            pl.BlockSpec((n_new, KV_PAGES * PAGE), lambda b, p, pt: (b, jnp.minimum(p, n_steps - 1))),
            pl.BlockSpec((n_new, PAGE), lambda b, p, pt: (b, n_pages)),
            pl.BlockSpec((None, 128, 128), seq3)],
        out_specs=pl.BlockSpec((None, 128, 128), seq3),
        scratch_shapes=[pltpu.VMEM((128, 128), F32), pltpu.VMEM((128, 128), F32),
                        pltpu.VMEM((128, 128), F32)],
    )
    return pl.pallas_call(
        functools.partial(_dsa_sample_kernel, n_steps=n_steps),
        out_shape=jax.ShapeDtypeStruct((nseq, 128, 128), BF16),
        grid_spec=grid_spec,
        compiler_params=_cp(("arbitrary", "arbitrary"), VMEM_LIMIT),
        name="dsa_sample",
    )(pt_flat, *([cache_k] * KV_PAGES), *([cache_v] * KV_PAGES), knew, vnew, qbd, bias, bias, z)


def _moba_sample_kernel(pt_ref, *refs, n_steps, n_blocks, n_new):
    kp_refs, vp_refs = refs[:KV_PAGES], refs[KV_PAGES:2 * KV_PAGES]
    (knew_ref, vnew_ref, q_ref, z_ref, o_ref,
     m_ref, l_ref, acc_ref, km_ref) = refs[2 * KV_PAGES:]
    p = pl.program_id(1)
    ppb = MOBA_BLOCK // PAGE
    blocks_per_step = KV_PAGES // ppb

    def partial_softmax(k, v, bias):
        s = lax.dot_general(q_ref[...], k, _NT, preferred_element_type=F32)
        if bias is not None:
            s = s + bias
        m = jnp.max(s, axis=-1, keepdims=True)
        pr = jnp.exp(s - m)
        l = jnp.sum(pr, axis=-1, keepdims=True)
        pv = jnp.dot(pr.astype(BF16), v, preferred_element_type=F32)
        shape = (128, 128)
        return jnp.broadcast_to(m, shape), jnp.broadcast_to(l, shape), _diag_blocks(pv, B_KV)

    @pl.when(p == 0)
    def _():
        km_ref[...] = jnp.zeros(km_ref.shape, F32)

    @pl.when(p < n_steps)
    def _():
        for u in range(blocks_per_step):
            n = p * blocks_per_step + u
            k32 = _kv_rows(kp_refs[u * ppb:(u + 1) * ppb], B_KV)
            v = _kv_rows(vp_refs[u * ppb:(u + 1) * ppb], B_KV).astype(BF16)
            km_ref[pl.ds(n, 1), :] = jnp.sum(k32, axis=0, keepdims=True) * (1.0 / MOBA_BLOCK)
            m_ref[n], l_ref[n], acc_ref[n] = partial_softmax(k32.astype(BF16), v, None)

    @pl.when(p == n_steps)
    def _():
        gate = lax.dot_general(q_ref[...], km_ref[...].astype(BF16), _NT, preferred_element_type=F32)
        sel = _top_blocks(gate, n_blocks)
        lane = lax.broadcasted_iota(I32, (128, 128), 1)
        t = lax.broadcasted_iota(I32, (128, 128), 0) % n_new
        m_own, l_own, acc_own = partial_softmax(_new_rows(knew_ref), _new_rows(vnew_ref),
                                                jnp.where(lane <= t, 0.0, NEG))
        picked = [jnp.broadcast_to(sel[:, n:n + 1], (128, 128)) > 0.0 for n in range(n_blocks)]
        m_all = m_own
        for n in range(n_blocks):
            m_all = jnp.maximum(m_all, jnp.where(picked[n], m_ref[n], NEG))
        w = jnp.exp(m_own - m_all)
        num = w * acc_own
        den = w * l_own
        for n in range(n_blocks):
            w = jnp.where(picked[n], jnp.exp(m_ref[n] - m_all), 0.0)
            num = num + w * acc_ref[n]
            den = den + w * l_ref[n]
        o_ref[...] = (num / den * _silu(z_ref[...])).astype(BF16)


def _moba_sample(pt_flat, cache_k, cache_v, knew, vnew, qbd, z, nseq, n_pages, n_new):
    n_steps = n_pages // KV_PAGES
    n_blocks = n_pages * PAGE // MOBA_BLOCK
    seq3 = lambda b, p, pt: (b, 0, 0)
    page_block = (None, PAGE * B_KV, HEAD_DIM)
    grid_spec = pltpu.PrefetchScalarGridSpec(
        num_scalar_prefetch=1,
        grid=(nseq, n_steps + 1),
        in_specs=_page_specs(page_block, n_pages, KV_PAGES) + _page_specs(page_block, n_pages, KV_PAGES) + [
            pl.BlockSpec((n_new, 1024), lambda b, p, pt: (b, 0)),
            pl.BlockSpec((n_new, 1024), lambda b, p, pt: (b, 0)),
            pl.BlockSpec((None, 128, 1024), seq3),
            pl.BlockSpec((None, 128, 128), seq3)],
        out_specs=pl.BlockSpec((None, 128, 128), seq3),
        scratch_shapes=[pltpu.VMEM((n_blocks, 128, 128), F32), pltpu.VMEM((n_blocks, 128, 128), F32),
                        pltpu.VMEM((n_blocks, 128, 128), F32), pltpu.VMEM((128, 1024), F32)],
    )
    return pl.pallas_call(
        functools.partial(_moba_sample_kernel, n_steps=n_steps, n_blocks=n_blocks, n_new=n_new),
        out_shape=jax.ShapeDtypeStruct((nseq, 128, 128), BF16),
        grid_spec=grid_spec,
        compiler_params=_cp(("arbitrary", "arbitrary"), VMEM_LIMIT),
        name="moba_sample",
    )(pt_flat, *([cache_k] * KV_PAGES), *([cache_v] * KV_PAGES), knew, vnew, qbd, z)


def _rope_tables(pos):
    half = HEAD_DIM // 2
    inv_freq = ROPE_THETA ** (-jnp.arange(half, dtype=F32) / half)
    ang = pos.astype(F32)[:, None] * inv_freq[None, :]
    cos, sin = jnp.cos(ang), jnp.sin(ang)
    return jnp.concatenate([cos, cos], axis=-1), jnp.concatenate([-sin, sin], axis=-1)


IDX_COLS = 160
REPACK_ROWS = 64


def _pack_kernel(w_ref, main_ref, small_ref):
    main_ref[:, 0:OFF_AZ] = w_ref[:, 0:OFF_AZ].astype(BF16)
    main_ref[:, OFF_AZ:] = w_ref[:, OFF_AZ + IDX_COLS:].astype(BF16)
    small_ref[...] = w_ref[:, OFF_AZ:OFF_AZ + N_SMALL].astype(BF16)


def _pack_w_in(w):
    _, k, n_in = w.shape
    return pl.pallas_call(
        _pack_kernel,
        out_shape=(jax.ShapeDtypeStruct((k, N_MAIN), BF16), jax.ShapeDtypeStruct((k, N_SMALL), BF16)),
        grid=(k // REPACK_ROWS,),
        in_specs=[pl.BlockSpec((None, REPACK_ROWS, n_in), lambda i: (0, i, 0))],
        out_specs=(pl.BlockSpec((REPACK_ROWS, N_MAIN), lambda i: (i, 0)),
                   pl.BlockSpec((REPACK_ROWS, N_SMALL), lambda i: (i, 0))),
        compiler_params=_cp(("parallel",), VMEM_LIMIT),
        name="pack_w_in",
    )(w)


def _to_grt(u, nseq, n_new, n_groups):
    u = u.reshape(nseq, n_new, n_groups, 2, HEAD_DIM)
    return jnp.transpose(u, (0, 2, 3, 1, 4)).reshape(nseq, n_groups * 2 * n_new, HEAD_DIM)


def _from_grt(u, nseq, n_new, n_groups):
    u = u.reshape(nseq, n_groups, 2, n_new, HEAD_DIM)
    return jnp.transpose(u, (0, 3, 1, 2, 4)).reshape(nseq * n_new, n_groups * 2 * HEAD_DIM)


def _block_diag_q(q, nseq, n_new, n_groups):
    qg = _to_grt(q, nseq, n_new, n_groups).reshape(nseq, n_groups, 2 * n_new, 1, HEAD_DIM)
    eye = jnp.eye(n_groups, dtype=q.dtype).reshape(1, n_groups, 1, n_groups, 1)
    return (qg * eye).reshape(nseq, n_groups * 2 * n_new, n_groups * HEAD_DIM)


def kernel(x_prompt, x_sample, cache_a_k, cache_a_v, cache_idx_k, cache_b_k, cache_b_v, page_table,
           norm_gain, w_in, q_norm_a, k_norm_a, q_norm_b, k_norm_b, w_out):
    nb, seq, _ = x_prompt.shape
    nseq, n_new, _ = x_sample.shape
    n_pages = page_table.shape[1]
    past = n_pages * PAGE
    n_pool = cache_a_k.shape[1]
    assert norm_gain.shape[0] == 1 and seq % TK == 0 and 16 * n_new == 128 and nseq * n_new == TQ
    assert n_pages % IDX_PAGES == 0 and n_pages % KV_PAGES == 0
    assert n_pages * PAGE // MOBA_BLOCK <= 128 and seq // MOBA_BLOCK <= 128

    w_main, w_small = _pack_w_in(w_in)
    w_o = w_out[0].astype(BF16)
    gains = (q_norm_a, k_norm_a, q_norm_b, k_norm_b)

    xp = x_prompt.reshape(nb * seq, D_MODEL)
    xs = x_sample.reshape(nseq * n_new, D_MODEL)
    ms = xs.shape[0]

    mp = nb * seq
    h = _rmsnorm(xp, xs, norm_gain, ms)
    tm = (mp + ms) // 8
    proj = _matmul(h, w_main, tm, 1024, "proj")
    small = _matmul(h, w_small, tm, N_SMALL, "proj_small")

    cos_p, sin_p = _rope_tables(jnp.arange(seq))
    cos_s, sin_s = _rope_tables(past + jnp.arange(ms) % n_new)
    (qa_p, ka32_p, ka_p, va32_p, va_p, iq_p, ik32_p, ik_p, iw_p,
     qb_p, kb32_p, kb_p, vb32_p, vb_p, kmean_p) = _post(proj, small, cos_p, sin_p, *gains, mp, 0)
    (qa_s, ka32_s, _, va32_s, _, iq_s, ik32_s, _, iw_s,
     qb_s, kb32_s, _, vb32_s, _, _) = _post(proj, small, cos_s, sin_s, *gains, ms, mp)

    bra_p = _dsa_prompt(iq_p, iw_p, ik_p, qa_p, ka_p, va_p, proj, nb, seq)
    nblk = seq // MOBA_BLOCK
    km_p = jnp.pad(kmean_p.reshape(nb, nblk, 1024), ((0, 0), (0, 128 - nblk), (0, 0)))
    brb_p = _moba_prompt(qb_p, kb_p, vb_p, km_p, proj, nb, seq)
    y_p = _merge(xp, bra_p, brb_p, w_o, proj, 1024, 512, 0)

    pt_flat = page_table.reshape(-1).astype(I32)
    c_idx = cache_idx_k.reshape(n_pool, PAGE, IDX_DIM)
    pool = lambda c: c.reshape(n_pool, PAGE * c.shape[-2], HEAD_DIM)

    iq_ht = jnp.transpose(iq_s.reshape(nseq, n_new, IDX_HEADS, IDX_DIM), (0, 2, 1, 3))
    iq_ht = iq_ht.reshape(nseq, IDX_HEADS * n_new, IDX_DIM)
    scores = _idx_sample(pt_flat, c_idx, ik32_s, iq_ht, iw_s, nseq, n_pages, n_new)
    bias_s = _thresh(scores, past, n_new, 64)
    za = _to_grt(proj[mp:, OFF_AZ:OFF_AZ + BRANCH_W], nseq, n_new, A_KV)
    bra_s = _dsa_sample(pt_flat, pool(cache_a_k), pool(cache_a_v), ka32_s, va32_s,
                        _block_diag_q(qa_s, nseq, n_new, A_KV), bias_s, za, nseq, n_pages, n_new)
    bra_s = _from_grt(bra_s, nseq, n_new, A_KV)

    zb = _to_grt(proj[mp:, OFF_BZ:OFF_BZ + BRANCH_W], nseq, n_new, B_KV)
    brb_s = _moba_sample(pt_flat, pool(cache_b_k), pool(cache_b_v), kb32_s, vb32_s,
                         _block_diag_q(qb_s, nseq, n_new, B_KV), zb, nseq, n_pages, n_new)
    brb_s = _from_grt(brb_s, nseq, n_new, B_KV)
    y_s = _merge(xs, bra_s, brb_s, w_o, proj, ms, 512, mp)

    kv = lambda u, b, t: u.reshape(1, b, t, 8, HEAD_DIM)
    ix = lambda u, b, t: u.reshape(1, b, t, IDX_DIM)
    return (y_p.reshape(nb, seq, D_MODEL), y_s.reshape(nseq, n_new, D_MODEL),
            kv(ka32_p, nb, seq), kv(va32_p, nb, seq), ix(ik32_p, nb, seq),
            kv(kb32_p, nb, seq), kv(vb32_p, nb, seq),
            kv(ka32_s, nseq, n_new), kv(va32_s, nseq, n_new), ix(ik32_s, nseq, n_new),
            kv(kb32_s, nseq, n_new), kv(vb32_s, nseq, n_new))
```

```python
import functools

import jax
import jax.numpy as jnp
from jax import lax
from jax.experimental import pallas as pl
from jax.experimental.pallas import tpu as pltpu

F32 = jnp.float32
BF16 = jnp.bfloat16
I32 = jnp.int32

D_MODEL = 4096
HEAD_DIM = 128
A_HEADS = 16
A_KV = 8
IDX_HEADS = 32
IDX_DIM = 128
IDX_TOPK = 256
B_HEADS = 16
B_KV = 8
MOBA_BLOCK = 256
MOBA_TOPK = 3
BRANCH_W = 2048
ROPE_THETA = 10000.0
EPS = 1e-6
PAGE = 128

OFF_AQ, OFF_AK, OFF_AV, OFF_IQ, OFF_AZ = 0, 2048, 3072, 4096, 8192
OFF_BQ, OFF_BK, OFF_BV, OFF_BZ, OFF_GATE = 10240, 12288, 13312, 14336, 16384
N_MAIN = 24576
N_SMALL = 256

NEG = -1e30
INT_MIN = -(2 ** 31)
KEY_NEG_INF = -2139095041
TQ = 256
TK = 512
IDX_PAGES = 16
KV_PAGES = 8
VMEM_LIMIT = 56 * 1024 * 1024

_NT = (((1,), (1,)), ((), ()))


def _cp(sem, vmem=None):
    return pltpu.CompilerParams(dimension_semantics=sem, vmem_limit_bytes=vmem)


def _rmsnorm_kernel(x_ref, g_ref, o_ref):
    x = x_ref[...]
    ms = jnp.mean(x * x, axis=-1, keepdims=True)
    o_ref[...] = (x * lax.rsqrt(ms + EPS) * g_ref[...]).astype(BF16)


def _rmsnorm2_kernel(xa_ref, xb_ref, g_ref, o_ref, *, na):
    i = pl.program_id(0)

    @pl.when(i < na)
    def _():
        _rmsnorm_kernel(xa_ref, g_ref, o_ref)

    @pl.when(i >= na)
    def _():
        _rmsnorm_kernel(xb_ref, g_ref, o_ref)


def _rmsnorm(xa, xb, gain, tm):
    na, nb = xa.shape[0] // tm, xb.shape[0] // tm
    return pl.pallas_call(
        functools.partial(_rmsnorm2_kernel, na=na),
        out_shape=jax.ShapeDtypeStruct(((na + nb) * tm, D_MODEL), BF16),
        grid=(na + nb,),
        in_specs=[pl.BlockSpec((tm, D_MODEL), lambda i: (jnp.minimum(i, na - 1), 0)),
                  pl.BlockSpec((tm, D_MODEL), lambda i: (jnp.maximum(i - na, 0), 0)),
                  pl.BlockSpec((1, D_MODEL), lambda i: (0, 0))],
        out_specs=pl.BlockSpec((tm, D_MODEL), lambda i: (i, 0)),
        compiler_params=_cp(("arbitrary",)),
        name="rmsnorm",
    )(xa, xb, gain)


def _matmul_kernel(x_ref, w_ref, o_ref):
    o_ref[...] = lax.dot_general(x_ref[...], w_ref[...], _NT, preferred_element_type=F32)


def _matmul(x, w, tm, tn, name):
    m, k = x.shape
    n = w.shape[0]
    return pl.pallas_call(
        _matmul_kernel,
        out_shape=jax.ShapeDtypeStruct((m, n), F32),
        grid=(m // tm, n // tn),
        in_specs=[pl.BlockSpec((tm, k), lambda i, j: (i, 0)),
                  pl.BlockSpec((tn, k), lambda i, j: (j, 0))],
        out_specs=pl.BlockSpec((tm, tn), lambda i, j: (i, j)),
        compiler_params=_cp(("parallel", "parallel"), VMEM_LIMIT),
        name=name,
    )(x, w)


def _rope(y, cos, sin):
    return y * cos + pltpu.roll(y, 64, 1) * sin


def _head_norm(x, gain):
    ms = jnp.mean(x * x, axis=-1, keepdims=True)
    return x * lax.rsqrt(ms + EPS) * gain


def _post_kernel(pa_ref, piq_ref, pbq_ref, pbkv_ref, ps_ref, cos_ref, sin_ref,
                 qna_ref, kna_ref, qnb_ref, knb_ref,
                 qa_ref, ka32_ref, ka_ref, va32_ref, va_ref, iq_ref, ik32_ref, ik_ref, iw_ref,
                 qb_ref, kb32_ref, kb_ref, vb32_ref, vb_ref, kmean_ref):
    cos = cos_ref[...]
    sin = sin_ref[...]
    scale = HEAD_DIM ** -0.5
    hs = lambda h: slice(h * HEAD_DIM, (h + 1) * HEAD_DIM)

    qna, kna, qnb, knb = qna_ref[...], kna_ref[...], qnb_ref[...], knb_ref[...]
    for h in range(A_HEADS):
        q = _rope(_head_norm(pa_ref[:, hs(h)], qna), cos, sin)
        qa_ref[:, hs(h)] = (q * scale).astype(BF16)
    for h in range(A_KV):
        k = _rope(_head_norm(pa_ref[:, hs(A_HEADS + h)], kna), cos, sin)
        ka32_ref[:, hs(h)] = k
        ka_ref[:, hs(h)] = k.astype(BF16)
        v = pa_ref[:, hs(A_HEADS + A_KV + h)]
        va32_ref[:, hs(h)] = v
        va_ref[:, hs(h)] = v.astype(BF16)
    for h in range(IDX_HEADS):
        iq_ref[:, hs(h)] = _rope(piq_ref[:, hs(h)], cos, sin).astype(BF16)
    ik = _rope(ps_ref[:, 0:IDX_DIM], cos, sin)
    ik32_ref[...] = ik
    ik_ref[...] = ik.astype(BF16)
    iw_ref[...] = ps_ref[:, IDX_DIM:2 * IDX_DIM] * (IDX_HEADS ** -0.5 * IDX_DIM ** -0.5)
    for h in range(B_HEADS):
        q = _rope(_head_norm(pbq_ref[:, hs(h)], qnb), cos, sin)
        qb_ref[:, hs(h)] = (q * scale).astype(BF16)
    for h in range(B_KV):
        k = _rope(_head_norm(pbkv_ref[:, hs(h)], knb), cos, sin)
        kb32_ref[:, hs(h)] = k
        kb_ref[:, hs(h)] = k.astype(BF16)
        kmean_ref[0, :, hs(h)] = jnp.mean(k, axis=0, keepdims=True)
        v = pbkv_ref[:, hs(B_KV + h)]
        vb32_ref[:, hs(h)] = v
        vb_ref[:, hs(h)] = v.astype(BF16)


def _post(proj, small, cos, sin, qna, kna, qnb, knb, m, row0):
    r = TQ
    nt = cos.shape[0] // r
    t0 = row0 // r
    row = lambda c: (lambda i: (i, c))
    src = lambda c: (lambda i: (t0 + i, c))
    gain = pl.BlockSpec((1, HEAD_DIM), lambda i: (0, 0))
    tab = pl.BlockSpec((r, HEAD_DIM), lambda i: (i % nt, 0))
    sds = jax.ShapeDtypeStruct
    kv32 = sds((m, 1024), F32)
    kvbf = sds((m, 1024), BF16)
    kvspec = pl.BlockSpec((r, 1024), row(0))
    out_shape = (sds((m, 2048), BF16), kv32, kvbf, kv32, kvbf,
                 sds((m, 4096), BF16), sds((m, 128), F32), sds((m, 128), BF16), sds((m, 128), F32),
                 sds((m, 2048), BF16), kv32, kvbf, kv32, kvbf,
                 sds((m // r, 1, 1024), F32))
    s128 = pl.BlockSpec((r, 128), row(0))
    out_specs = (pl.BlockSpec((r, 2048), row(0)), kvspec, kvspec, kvspec, kvspec,
                 pl.BlockSpec((r, 4096), row(0)), s128, s128, s128,
                 pl.BlockSpec((r, 2048), row(0)), kvspec, kvspec, kvspec, kvspec,
                 pl.BlockSpec((1, 1, 1024), lambda i: (i, 0, 0)))
    return pl.pallas_call(
        _post_kernel,
        out_shape=out_shape,
        grid=(m // r,),
        in_specs=[pl.BlockSpec((r, 4096), src(0)),
                  pl.BlockSpec((r, 4096), src(1)),
                  pl.BlockSpec((r, 2048), src(OFF_BQ // 2048)),
                  pl.BlockSpec((r, 2048), src(OFF_BK // 2048)),
                  pl.BlockSpec((r, N_SMALL), src(0)),
                  tab, tab, gain, gain, gain, gain],
        out_specs=out_specs,
        compiler_params=_cp(("parallel",), VMEM_LIMIT),
        name="post",
    )(proj, proj, proj, proj, small, cos, sin, qna, kna, qnb, knb)


def _key_to_float(key):
    bits = jnp.where(key >= 0, key, key ^ jnp.int32(0x7FFFFFFF))
    return lax.bitcast_convert_type(bits, F32)


def _kth_largest_key(count_ge, shape):
    def bit_body(b, thr):
        cand = thr + lax.shift_left(jnp.int32(1), 31 - b)
        n = count_ge(_key_to_float(cand))
        return jnp.where(cand <= KEY_NEG_INF, cand, jnp.where(n >= IDX_TOPK, cand, thr))

    return lax.fori_loop(0, 32, bit_body, jnp.full(shape, INT_MIN, I32))


def _silu(z):
    return z * jax.nn.sigmoid(z)


def _rep(x, n, axis):
    return jnp.concatenate([x] * n, axis=axis)


ROW_BLOCK = 64


def _attend_init(m_ref, l_ref, acc_ref):
    m_ref[...] = jnp.full(m_ref.shape, NEG, F32)
    l_ref[...] = jnp.zeros(l_ref.shape, F32)
    acc_ref[...] = jnp.zeros(acc_ref.shape, F32)


def _attend_tile(q2, k, v, bias_rows, s_ref, p_ref, m_ref, l_ref, acc_ref):
    s_ref[...] = lax.dot_general(q2, k, _NT, preferred_element_type=F32)
    reps = k.shape[0] // 128
    for b in range(q2.shape[0] // ROW_BLOCK):
        rs = slice(b * ROW_BLOCK, (b + 1) * ROW_BLOCK)
        s = s_ref[rs, :] + bias_rows(b)
        m_prev = m_ref[rs, :]
        m_new = jnp.maximum(m_prev, jnp.max(s, axis=-1, keepdims=True))
        alpha = jnp.exp(m_prev - m_new)
        p = jnp.exp(s - _rep(m_new, reps, 1))
        l_ref[rs, :] = alpha * l_ref[rs, :] + jnp.sum(p, axis=-1, keepdims=True)
        m_ref[rs, :] = m_new
        acc_ref[rs, :] = acc_ref[rs, :] * alpha
        p_ref[rs, :] = p.astype(BF16)
    acc_ref[...] += jnp.dot(p_ref[...], v, preferred_element_type=F32)


def _dsa_prompt_kernel(iq_ref, iw_ref, ik_ref, qa_ref, ka_ref, va_ref, az_ref, o_ref,
                       wb_ref, sc_ref, bias_ref, cut_ref, s_ref, p_ref, m_ref, l_ref, acc_ref):
    i = pl.program_id(1)
    g = pl.program_id(2)
    n_tiles = i + 1
    half = TQ // 2
    lane = lax.broadcasted_iota(I32, (TQ, 128), 1)
    rowg = lax.broadcasted_iota(I32, (TQ, 128), 0) + i * TQ

    @pl.when(g == 0)
    def _select():
        for h in range(IDX_HEADS):
            wb_ref[h] = jnp.broadcast_to(iw_ref[:, h:h + 1], (TQ, 128))

        def score_tile(j, _):
            ks = ik_ref[pl.ds(pl.multiple_of(j * TQ, TQ), TQ), :]
            for rh in range(2):
                rs = slice(rh * half, (rh + 1) * half)
                acc0 = jnp.zeros((half, 128), F32)
                acc1 = jnp.zeros((half, 128), F32)
                for h in range(IDX_HEADS):
                    lg = lax.dot_general(iq_ref[rs, h * 128:(h + 1) * 128], ks, _NT,
                                         preferred_element_type=F32)
                    lg = jnp.maximum(lg, 0.0)
                    w = wb_ref[h, rs, :]
                    acc0 = acc0 + lg[:, :128] * w
                    acc1 = acc1 + lg[:, 128:] * w
                colg = lax.broadcasted_iota(I32, (half, 128), 1) + j * TQ
                rg = lax.broadcasted_iota(I32, (half, 128), 0) + (i * TQ + rh * half)
                sc_ref[j, rs, 0:128] = jnp.where(colg <= rg, acc0, -jnp.inf)
                sc_ref[j, rs, 128:256] = jnp.where(colg + 128 <= rg, acc1, -jnp.inf)
            return 0

        lax.fori_loop(0, n_tiles, score_tile, 0)

        def count(pred):
            def body(j, cnt):
                t = sc_ref[j]
                colg = lane + j * TQ
                return (cnt + jnp.where(pred(t[:, :128], colg), 1.0, 0.0)
                        + jnp.where(pred(t[:, 128:], colg + 128), 1.0, 0.0))
            cnt = lax.fori_loop(0, n_tiles, body, jnp.zeros((TQ, 128), F32))
            return jnp.sum(cnt, axis=-1, keepdims=True)

        thr = _key_to_float(_kth_largest_key(lambda c: count(lambda t, _: t >= c), (TQ, 128)))
        need = IDX_TOPK - count(lambda t, _: t > thr)
        n_tie = count(lambda t, _: t == thr)

        cut_ref[...] = jnp.full((TQ, 128), 4096, I32)

        @pl.when(jnp.max(n_tie - need) > 0.0)
        def _ties():
            def cut_body(b, cut):
                cand = cut + lax.shift_left(jnp.int32(1), 11 - b)
                n = count(lambda t, c: jnp.where(t == thr, c, 4096) < cand)
                return jnp.where(n < need, cand, cut)
            cut_ref[...] = lax.fori_loop(0, 12, cut_body, jnp.zeros((TQ, 128), I32))

        cut = cut_ref[...]

        def bias_tile(j, _):
            t = sc_ref[j]
            for hh in range(2):
                th = t[:, hh * 128:(hh + 1) * 128]
                colg = lane + (j * TQ + hh * 128)
                tie = jnp.where(th == thr, jnp.where(colg <= cut, 0.0, NEG), NEG)
                picked = jnp.where(th > thr, 0.0, tie)
                bias_ref[j, :, hh * 128:(hh + 1) * 128] = jnp.where(colg <= rowg, picked, NEG)
            return 0

        lax.fori_loop(0, n_tiles, bias_tile, 0)

        @pl.when(i % 2 == 0)
        def _pad_tile():
            bias_ref[i + 1] = jnp.full((TQ, TQ), NEG, F32)

    q2 = jnp.concatenate([qa_ref[:, 0:128], qa_ref[:, 128:256]], axis=0)
    _attend_init(m_ref, l_ref, acc_ref)

    def body(jj, _):
        off = pl.multiple_of(jj * TK, TK)

        def bias_rows(b):
            rs = slice(b * ROW_BLOCK % TQ, b * ROW_BLOCK % TQ + ROW_BLOCK)
            return jnp.concatenate([bias_ref[2 * jj, rs, :], bias_ref[2 * jj + 1, rs, :]], axis=1)

        _attend_tile(q2, ka_ref[pl.ds(off, TK), :], va_ref[pl.ds(off, TK), :], bias_rows,
                     s_ref, p_ref, m_ref, l_ref, acc_ref)
        return 0

    lax.fori_loop(0, (i + 2) // 2, body, 0)
    for r in range(2):
        cs = slice(r * 128, (r + 1) * 128)
        rs = slice(r * TQ, (r + 1) * TQ)
        o_ref[:, cs] = (acc_ref[rs, :] / l_ref[rs, :] * _silu(az_ref[:, cs])).astype(BF16)


def _dsa_prompt(iq, iw, ik, qa, ka, va, proj, nb, seq):
    nq = seq // TQ
    m = nb * seq
    rowt = lambda b, i, g: b * nq + i
    return pl.pallas_call(
        _dsa_prompt_kernel,
        out_shape=jax.ShapeDtypeStruct((m, BRANCH_W), BF16),
        grid=(nb, nq, A_KV),
        in_specs=[pl.BlockSpec((TQ, 4096), lambda b, i, g: (rowt(b, i, g), 0)),
                  pl.BlockSpec((TQ, 128), lambda b, i, g: (rowt(b, i, g), 0)),
                  pl.BlockSpec((seq, 128), lambda b, i, g: (b, 0)),
                  pl.BlockSpec((TQ, 256), lambda b, i, g: (rowt(b, i, g), g)),
                  pl.BlockSpec((seq, 128), lambda b, i, g: (b, g)),
                  pl.BlockSpec((seq, 128), lambda b, i, g: (b, g)),
                  pl.BlockSpec((TQ, 256), lambda b, i, g: (rowt(b, i, g), OFF_AZ // 256 + g))],
        out_specs=pl.BlockSpec((TQ, 256), lambda b, i, g: (rowt(b, i, g), g)),
        scratch_shapes=[pltpu.VMEM((IDX_HEADS, TQ, 128), F32),
                        pltpu.VMEM((seq // TQ, TQ, TQ), F32),
                        pltpu.VMEM((seq // TQ, TQ, TQ), F32),
                        pltpu.VMEM((TQ, 128), I32),
                        pltpu.VMEM((2 * TQ, TK), F32), pltpu.VMEM((2 * TQ, TK), BF16),
                        pltpu.VMEM((2 * TQ, 128), F32), pltpu.VMEM((2 * TQ, 128), F32),
                        pltpu.VMEM((2 * TQ, 128), F32)],
        compiler_params=_cp(("arbitrary", "arbitrary", "arbitrary"), VMEM_LIMIT),
        name="dsa_prompt",
    )(iq, iw, ik, qa, ka, va, proj)


def _top_blocks(gate, n_valid):
    lane = lax.broadcasted_iota(I32, gate.shape, 1).astype(F32)
    valid = lane < n_valid
    rem = jnp.where(valid, gate, -jnp.inf)
    sel = jnp.zeros(gate.shape, F32)
    for _ in range(MOBA_TOPK):
        mx = jnp.max(rem, axis=-1, keepdims=True)
        first = jnp.min(jnp.where(rem == mx, lane, 128.0), axis=-1, keepdims=True)
        pick = lane == first
        sel = jnp.where(pick, 1.0, sel)
        rem = jnp.where(pick, -jnp.inf, rem)
    return jnp.where(valid, sel, 0.0)


def _block_bias(sel, lane, n):
    picked = jnp.max(jnp.where(lane == n, sel, 0.0), axis=-1, keepdims=True)
    return jnp.where(picked > 0.0, 0.0, NEG)


def _moba_prompt_kernel(qb_ref, kb_ref, vb_ref, km_ref, bz_ref, o_ref, s_ref, p_ref, m_ref, l_ref, acc_ref):
    i = pl.program_id(1)
    f = i // 2
    lane = lax.broadcasted_iota(I32, (TQ, 128), 1)
    row = lax.broadcasted_iota(I32, (TQ, TQ), 0)
    col = lax.broadcasted_iota(I32, (TQ, TQ), 1)
    own_bias = jnp.where(col <= row, 0.0, NEG)
    neg_tile = jnp.full((TQ, TQ), NEG, F32)
    km = km_ref[...].astype(BF16)

    q2 = jnp.concatenate([qb_ref[:, 0:128], qb_ref[:, 128:256]], axis=0)
    gate = lax.dot_general(q2, km, _NT, preferred_element_type=F32)
    sels = [_top_blocks(gate[r * TQ:(r + 1) * TQ], i) for r in range(2)]
    _attend_init(m_ref, l_ref, acc_ref)
    per_head = TQ // ROW_BLOCK

    def kv(jj):
        off = pl.multiple_of(jj * TK, TK)
        return kb_ref[pl.ds(off, TK), :], vb_ref[pl.ds(off, TK), :]

    def body(jj, _):
        cols = [[_block_bias(sels[r], lane, 2 * jj + u) for u in range(2)] for r in range(2)]

        def bias_rows(b):
            r, rs = b // per_head, slice(b % per_head * ROW_BLOCK, (b % per_head + 1) * ROW_BLOCK)
            return jnp.concatenate([jnp.broadcast_to(c[rs], (ROW_BLOCK, TQ)) for c in cols[r]], axis=1)

        _attend_tile(q2, *kv(jj), bias_rows, s_ref, p_ref, m_ref, l_ref, acc_ref)
        return 0

    lax.fori_loop(0, f, body, 0)

    even = i % 2 == 0
    last = []
    for r in range(2):
        past = jnp.broadcast_to(_block_bias(sels[r], lane, 2 * f), (TQ, TQ))
        last.append(jnp.concatenate([jnp.where(even, own_bias, past),
                                     jnp.where(even, neg_tile, own_bias)], axis=1))

    def last_rows(b):
        r = b // per_head
        return last[r][b % per_head * ROW_BLOCK:(b % per_head + 1) * ROW_BLOCK]

    _attend_tile(q2, *kv(f), last_rows, s_ref, p_ref, m_ref, l_ref, acc_ref)
    for r in range(2):
        cs = slice(r * 128, (r + 1) * 128)
        rs = slice(r * TQ, (r + 1) * TQ)
        o_ref[:, cs] = (acc_ref[rs, :] / l_ref[rs, :] * _silu(bz_ref[:, cs])).astype(BF16)


def _moba_prompt(qb, kb, vb, kmean, proj, nb, seq):
    nq = seq // TQ
    m = nb * seq
    rowt = lambda b, i, g: b * nq + i
    return pl.pallas_call(
        _moba_prompt_kernel,
        out_shape=jax.ShapeDtypeStruct((m, BRANCH_W), BF16),
        grid=(nb, nq, B_KV),
        in_specs=[pl.BlockSpec((TQ, 256), lambda b, i, g: (rowt(b, i, g), g)),
                  pl.BlockSpec((seq, 128), lambda b, i, g: (b, g)),
                  pl.BlockSpec((seq, 128), lambda b, i, g: (b, g)),
                  pl.BlockSpec((None, 128, 128), lambda b, i, g: (b, 0, g)),
                  pl.BlockSpec((TQ, 256), lambda b, i, g: (rowt(b, i, g), OFF_BZ // 256 + g))],
        out_specs=pl.BlockSpec((TQ, 256), lambda b, i, g: (rowt(b, i, g), g)),
        scratch_shapes=[pltpu.VMEM((2 * TQ, TK), F32), pltpu.VMEM((2 * TQ, TK), BF16),
                        pltpu.VMEM((2 * TQ, 128), F32), pltpu.VMEM((2 * TQ, 128), F32),
                        pltpu.VMEM((2 * TQ, 128), F32)],
        compiler_params=_cp(("arbitrary", "arbitrary", "arbitrary"), VMEM_LIMIT),
        name="moba_prompt",
    )(qb, kb, vb, kmean, proj)


def _merge_kernel(x_ref, a_ref, b_ref, wa_ref, wb_ref, ga_ref, gb_ref, o_ref):
    pa = jnp.dot(a_ref[...], wa_ref[...], preferred_element_type=F32)
    pb = jnp.dot(b_ref[...], wb_ref[...], preferred_element_type=F32)
    o_ref[...] = x_ref[...] + (jax.nn.sigmoid(ga_ref[...]) * pa + jax.nn.sigmoid(gb_ref[...]) * pb)


def _merge(x, bra, brb, w_out, proj, tm, tn, row0):
    m = x.shape[0]
    t0 = row0 // tm
    ga0 = OFF_GATE // tn
    gb0 = (OFF_GATE + D_MODEL) // tn
    return pl.pallas_call(
        _merge_kernel,
        out_shape=jax.ShapeDtypeStruct((m, D_MODEL), F32),
        grid=(m // tm, D_MODEL // tn),
        in_specs=[pl.BlockSpec((tm, tn), lambda i, j: (i, j)),
                  pl.BlockSpec((tm, BRANCH_W), lambda i, j: (i, 0)),
                  pl.BlockSpec((tm, BRANCH_W), lambda i, j: (i, 0)),
                  pl.BlockSpec((None, BRANCH_W, tn), lambda i, j: (0, 0, j)),
                  pl.BlockSpec((None, BRANCH_W, tn), lambda i, j: (1, 0, j)),
                  pl.BlockSpec((tm, tn), lambda i, j: (t0 + i, ga0 + j)),
                  pl.BlockSpec((tm, tn), lambda i, j: (t0 + i, gb0 + j))],
        out_specs=pl.BlockSpec((tm, tn), lambda i, j: (i, j)),
        compiler_params=_cp(("parallel", "parallel"), VMEM_LIMIT),
        name="merge",
    )(x, bra, brb, w_out, w_out, proj, proj)


def _page_specs(block, n_pages, per_step):
    def spec(r):
        def index_map(b, p, pt):
            page = jnp.minimum(p * per_step + r, n_pages - 1)
            return (pt[b * n_pages + page],) + (0,) * (len(block) - 1)
        return pl.BlockSpec(block, index_map)
    return [spec(r) for r in range(per_step)]


def _new_rows(ref):
    n_new, width = ref.shape
    return jnp.concatenate([ref[...], jnp.zeros((PAGE - n_new, width), F32)], axis=0).astype(BF16)


def _kv_rows(refs, n_heads):
    def head(ref, g):
        return ref[pl.ds(g, PAGE, stride=n_heads), :]
    return jnp.concatenate(
        [jnp.concatenate([head(ref, g) for g in range(n_heads)], axis=1) for ref in refs], axis=0)


def _idx_sample_kernel(pt_ref, *refs, n_steps, n_new):
    kp_refs = refs[:IDX_PAGES]
    knew_ref, iq_ref, iw_ref, o_ref = refs[IDX_PAGES:]
    p = pl.program_id(1)

    def scores(keys):
        lg = lax.dot_general(iq_ref[...], keys, _NT, preferred_element_type=F32)
        acc = jnp.zeros((n_new, keys.shape[0]), F32)
        for h in range(IDX_HEADS):
            w = jnp.broadcast_to(iw_ref[:, h:h + 1], acc.shape)
            acc = acc + jnp.maximum(lg[h * n_new:(h + 1) * n_new, :], 0.0) * w
        return acc

    @pl.when(p < n_steps)
    def _():
        keys = jnp.concatenate([r[...] for r in kp_refs], axis=0).astype(BF16)
        o_ref[...] = scores(keys)

    @pl.when(p == n_steps)
    def _():
        acc = scores(_new_rows(knew_ref))
        t = lax.broadcasted_iota(I32, acc.shape, 0)
        c = lax.broadcasted_iota(I32, acc.shape, 1)
        o_ref[:, 0:PAGE] = jnp.where(c <= t, acc, -jnp.inf)
        o_ref[:, PAGE:] = jnp.full((n_new, o_ref.shape[1] - PAGE), -jnp.inf, F32)


def _idx_sample(pt_flat, cache_idx, knew_pad, iq_ht, iw, nseq, n_pages, n_new):
    n_steps = n_pages // IDX_PAGES
    width = IDX_PAGES * PAGE
    grid_spec = pltpu.PrefetchScalarGridSpec(
        num_scalar_prefetch=1,
        grid=(nseq, n_steps + 1),
        in_specs=_page_specs((None, PAGE, IDX_DIM), n_pages, IDX_PAGES) + [
            pl.BlockSpec((n_new, IDX_DIM), lambda b, p, pt: (b, 0)),
            pl.BlockSpec((None, IDX_HEADS * n_new, IDX_DIM), lambda b, p, pt: (b, 0, 0)),
            pl.BlockSpec((n_new, 128), lambda b, p, pt: (b, 0))],
        out_specs=pl.BlockSpec((n_new, width), lambda b, p, pt: (b, p)),
    )
    return pl.pallas_call(
        functools.partial(_idx_sample_kernel, n_steps=n_steps, n_new=n_new),
        out_shape=jax.ShapeDtypeStruct((nseq * n_new, (n_steps + 1) * width), F32),
        grid_spec=grid_spec,
        compiler_params=_cp(("arbitrary", "arbitrary")),
        name="idx_sample",
    )(pt_flat, *([cache_idx] * IDX_PAGES), knew_pad, iq_ht, iw)


def _thresh_kernel(s_ref, o_ref, *, n_past, n_new):
    rows, width = s_ref.shape
    s = s_ref[...]
    col = lax.broadcasted_iota(I32, (rows, width), 1)
    t_new = lax.broadcasted_iota(I32, (rows, width), 0) % n_new

    def count(mask):
        return jnp.sum(jnp.where(mask, 1.0, 0.0), axis=-1, keepdims=True)

    thr = _key_to_float(_kth_largest_key(lambda c: count(s >= c), (rows, 1)))
    need = IDX_TOPK - count(s > thr)
    tie_col = jnp.where(s == thr, col, 2 ** 14)

    def cut_body(b, cut):
        cand = cut + lax.shift_left(jnp.int32(1), 13 - b)
        return jnp.where(count(tie_col < cand) < need, cand, cut)

    cut = lax.fori_loop(0, 14, cut_body, jnp.zeros((rows, 1), I32))
    tie = jnp.where(s == thr, jnp.where(col <= cut, 0.0, NEG), NEG)
    picked = jnp.where(s > thr, 0.0, tie)
    o_ref[...] = jnp.where(col <= n_past + t_new, picked, NEG)


def _thresh(scores, n_past, n_new, rows):
    m, width = scores.shape
    return pl.pallas_call(
        functools.partial(_thresh_kernel, n_past=n_past, n_new=n_new),
        out_shape=jax.ShapeDtypeStruct((m, width), F32),
        grid=(m // rows,),
        in_specs=[pl.BlockSpec((rows, width), lambda i: (i, 0))],
        out_specs=pl.BlockSpec((rows, width), lambda i: (i, 0)),
        compiler_params=_cp(("parallel",), VMEM_LIMIT),
        name="thresh_sample",
    )(scores)


def _diag_blocks(x, n_groups):
    rows = x.shape[0] // n_groups
    return jnp.concatenate(
        [x[g * rows:(g + 1) * rows, g * 128:(g + 1) * 128] for g in range(n_groups)], axis=0)


def _dsa_sample_kernel(pt_ref, *refs, n_steps):
    kp_refs, vp_refs = refs[:KV_PAGES], refs[KV_PAGES:2 * KV_PAGES]
    (knew_ref, vnew_ref, q_ref, bias_ref, bias_new_ref, z_ref, o_ref,
     m_ref, l_ref, acc_ref) = refs[2 * KV_PAGES:]
    p = pl.program_id(1)

    @pl.when(p == 0)
    def _():
        m_ref[...] = jnp.full(m_ref.shape, NEG, F32)
        l_ref[...] = jnp.zeros(l_ref.shape, F32)
        acc_ref[...] = jnp.zeros(acc_ref.shape, F32)

    def step(k, v, bias8):
        s = lax.dot_general(q_ref[...], k, _NT, preferred_element_type=F32)
        s = s + _rep(bias8, 128 // bias8.shape[0], 0)
        m_prev = m_ref[...]
        m_new = jnp.maximum(m_prev, jnp.max(s, axis=-1, keepdims=True))
        alpha = jnp.exp(m_prev - m_new)
        pr = jnp.exp(s - _rep(m_new, s.shape[1] // 128, 1))
        l_ref[...] = alpha * l_ref[...] + jnp.sum(pr, axis=-1, keepdims=True)
        m_ref[...] = m_new
        pv = jnp.dot(pr.astype(BF16), v, preferred_element_type=F32)
        acc_ref[...] = acc_ref[...] * alpha + _diag_blocks(pv, A_KV)

    @pl.when(p < n_steps)
    def _():
        step(_kv_rows(kp_refs, A_KV).astype(BF16), _kv_rows(vp_refs, A_KV).astype(BF16), bias_ref[...])

    @pl.when(p == n_steps)
    def _():
        step(_new_rows(knew_ref), _new_rows(vnew_ref), bias_new_ref[...])
        o_ref[...] = (acc_ref[...] / l_ref[...] * _silu(z_ref[...])).astype(BF16)


def _dsa_sample(pt_flat, cache_k, cache_v, knew, vnew, qbd, bias, z, nseq, n_pages, n_new):
    n_steps = n_pages // KV_PAGES
    seq3 = lambda b, p, pt: (b, 0, 0)
    page_block = (None, PAGE * A_KV, HEAD_DIM)
    grid_spec = pltpu.PrefetchScalarGridSpec(
        num_scalar_prefetch=1,
        grid=(nseq, n_steps + 1),
        in_specs=_page_specs(page_block, n_pages, KV_PAGES) + _page_specs(page_block, n_pages, KV_PAGES) + [
            pl.BlockSpec((n_new, 1024), lambda b, p, pt: (b, 0)),
            pl.BlockSpec((n_new, 1024), lambda b, p, pt: (b, 0)),
            pl.BlockSpec((None, 128, 1024), seq3),
            pl.BlockSpec((n_new, KV_PAGES * PAGE), lambda b, p, pt: (b, jnp.minimum(p, n_steps - 1))),
            pl.BlockSpec((n_new, PAGE), lambda b, p, pt: (b, n_pages)),
            pl.BlockSpec((None, 128, 128), seq3)],
        out_specs=pl.BlockSpec((None, 128, 128), seq3),
        scratch_shapes=[pltpu.VMEM((128, 128), F32), pltpu.VMEM((128, 128), F32),
                        pltpu.VMEM((128, 128), F32)],
    )
    return pl.pallas_call(
        functools.partial(_dsa_sample_kernel, n_steps=n_steps),
        out_shape=jax.ShapeDtypeStruct((nseq, 128, 128), BF16),
        grid_spec=grid_spec,
        compiler_params=_cp(("arbitrary", "arbitrary"), VMEM_LIMIT),
        name="dsa_sample",
    )(pt_flat, *([cache_k] * KV_PAGES), *([cache_v] * KV_PAGES), knew, vnew, qbd, bias, bias, z)


def _moba_sample_kernel(pt_ref, *refs, n_steps, n_blocks, n_new):
    kp_refs, vp_refs = refs[:KV_PAGES], refs[KV_PAGES:2 * KV_PAGES]
    (knew_ref, vnew_ref, q_ref, z_ref, o_ref,
     m_ref, l_ref, acc_ref, km_ref) = refs[2 * KV_PAGES:]
    p = pl.program_id(1)
    ppb = MOBA_BLOCK // PAGE
    blocks_per_step = KV_PAGES // ppb

    def partial_softmax(k, v, bias):
        s = lax.dot_general(q_ref[...], k, _NT, preferred_element_type=F32)
        if bias is not None:
            s = s + bias
        m = jnp.max(s, axis=-1, keepdims=True)
        pr = jnp.exp(s - m)
        l = jnp.sum(pr, axis=-1, keepdims=True)
        pv = jnp.dot(pr.astype(BF16), v, preferred_element_type=F32)
        shape = (128, 128)
        return jnp.broadcast_to(m, shape), jnp.broadcast_to(l, shape), _diag_blocks(pv, B_KV)

    @pl.when(p == 0)
    def _():
        km_ref[...] = jnp.zeros(km_ref.shape, F32)

    @pl.when(p < n_steps)
    def _():
        for u in range(blocks_per_step):
            n = p * blocks_per_step + u
            k32 = _kv_rows(kp_refs[u * ppb:(u + 1) * ppb], B_KV)
            v = _kv_rows(vp_refs[u * ppb:(u + 1) * ppb], B_KV).astype(BF16)
            km_ref[pl.ds(n, 1), :] = jnp.sum(k32, axis=0, keepdims=True) * (1.0 / MOBA_BLOCK)
            m_ref[n], l_ref[n], acc_ref[n] = partial_softmax(k32.astype(BF16), v, None)

    @pl.when(p == n_steps)
    def _():
        gate = lax.dot_general(q_ref[...], km_ref[...].astype(BF16), _NT, preferred_element_type=F32)
        sel = _top_blocks(gate, n_blocks)
        lane = lax.broadcasted_iota(I32, (128, 128), 1)
        t = lax.broadcasted_iota(I32, (128, 128), 0) % n_new
        m_own, l_own, acc_own = partial_softmax(_new_rows(knew_ref), _new_rows(vnew_ref),
                                                jnp.where(lane <= t, 0.0, NEG))
        picked = [jnp.broadcast_to(sel[:, n:n + 1], (128, 128)) > 0.0 for n in range(n_blocks)]
        m_all = m_own
        for n in range(n_blocks):
            m_all = jnp.maximum(m_all, jnp.where(picked[n], m_ref[n], NEG))
        w = jnp.exp(m_own - m_all)
        num = w * acc_own
        den = w * l_own
        for n in range(n_blocks):
            w = jnp.where(picked[n], jnp.exp(m_ref[n] - m_all), 0.0)
            num = num + w * acc_ref[n]
            den = den + w * l_ref[n]
        o_ref[...] = (num / den * _silu(z_ref[...])).astype(BF16)


def _moba_sample(pt_flat, cache_k, cache_v, knew, vnew, qbd, z, nseq, n_pages, n_new):
    n_steps = n_pages // KV_PAGES
    n_blocks = n_pages * PAGE // MOBA_BLOCK
    seq3 = lambda b, p, pt: (b, 0, 0)
    page_block = (None, PAGE * B_KV, HEAD_DIM)
    grid_spec = pltpu.PrefetchScalarGridSpec(
        num_scalar_prefetch=1,
        grid=(nseq, n_steps + 1),
        in_specs=_page_specs(page_block, n_pages, KV_PAGES) + _page_specs(page_block, n_pages, KV_PAGES) + [
            pl.BlockSpec((n_new, 1024), lambda b, p, pt: (b, 0)),
            pl.BlockSpec((n_new, 1024), lambda b, p, pt: (b, 0)),
            pl.BlockSpec((None, 128, 1024), seq3),
            pl.BlockSpec((None, 128, 128), seq3)],
        out_specs=pl.BlockSpec((None, 128, 128), seq3),
        scratch_shapes=[pltpu.VMEM((n_blocks, 128, 128), F32), pltpu.VMEM((n_blocks, 128, 128), F32),
                        pltpu.VMEM((n_blocks, 128, 128), F32), pltpu.VMEM((128, 1024), F32)],
    )
    return pl.pallas_call(
        functools.partial(_moba_sample_kernel, n_steps=n_steps, n_blocks=n_blocks, n_new=n_new),
        out_shape=jax.ShapeDtypeStruct((nseq, 128, 128), BF16),
        grid_spec=grid_spec,
        compiler_params=_cp(("arbitrary", "arbitrary"), VMEM_LIMIT),
        name="moba_sample",
    )(pt_flat, *([cache_k] * KV_PAGES), *([cache_v] * KV_PAGES), knew, vnew, qbd, z)


def _rope_tables(pos):
    half = HEAD_DIM // 2
    inv_freq = ROPE_THETA ** (-jnp.arange(half, dtype=F32) / half)
    ang = pos.astype(F32)[:, None] * inv_freq[None, :]
    cos, sin = jnp.cos(ang), jnp.sin(ang)
    return jnp.concatenate([cos, cos], axis=-1), jnp.concatenate([-sin, sin], axis=-1)


IDX_COLS = 160
PACK_ROWS = 512


def _pack_main_kernel(a_ref, b_ref, o_ref):
    i = pl.program_id(0)

    @pl.when(i < OFF_AZ // PACK_ROWS)
    def _():
        o_ref[...] = a_ref[...].astype(BF16)

    @pl.when(i >= OFF_AZ // PACK_ROWS)
    def _():
        o_ref[0:PACK_ROWS - IDX_COLS, :] = a_ref[IDX_COLS:, :].astype(BF16)
        o_ref[PACK_ROWS - IDX_COLS:, :] = b_ref[0:IDX_COLS, :].astype(BF16)


def _pack_small_kernel(w_ref, o_ref):
    o_ref[...] = w_ref[...].astype(BF16)


def _pack_w_in(wt):
    _, n_in, k = wt.shape
    last = pl.cdiv(n_in, PACK_ROWS) - 1
    main = pl.pallas_call(
        _pack_main_kernel,
        out_shape=jax.ShapeDtypeStruct((N_MAIN, k), BF16),
        grid=(N_MAIN // PACK_ROWS,),
        in_specs=[pl.BlockSpec((None, PACK_ROWS, k), lambda i: (0, i, 0)),
                  pl.BlockSpec((None, PACK_ROWS, k), lambda i: (0, jnp.minimum(i + 1, last), 0))],
        out_specs=pl.BlockSpec((PACK_ROWS, k), lambda i: (i, 0)),
        compiler_params=_cp(("parallel",), VMEM_LIMIT),
        name="pack_w_in",
    )(wt, wt)
    small = pl.pallas_call(
        _pack_small_kernel,
        out_shape=jax.ShapeDtypeStruct((N_SMALL, k), BF16),
        grid=(1,),
        in_specs=[pl.BlockSpec((None, N_SMALL, k), lambda i: (0, OFF_AZ // N_SMALL, 0))],
        out_specs=pl.BlockSpec((N_SMALL, k), lambda i: (0, 0)),
        name="pack_w_idx",
    )(wt)
    return main, small


def _to_grt(u, nseq, n_new, n_groups):
    u = u.reshape(nseq, n_new, n_groups, 2, HEAD_DIM)
    return jnp.transpose(u, (0, 2, 3, 1, 4)).reshape(nseq, n_groups * 2 * n_new, HEAD_DIM)


def _from_grt(u, nseq, n_new, n_groups):
    u = u.reshape(nseq, n_groups, 2, n_new, HEAD_DIM)
    return jnp.transpose(u, (0, 3, 1, 2, 4)).reshape(nseq * n_new, n_groups * 2 * HEAD_DIM)


def _block_diag_q(q, nseq, n_new, n_groups):
    qg = _to_grt(q, nseq, n_new, n_groups).reshape(nseq, n_groups, 2 * n_new, 1, HEAD_DIM)
    eye = jnp.eye(n_groups, dtype=q.dtype).reshape(1, n_groups, 1, n_groups, 1)
    return (qg * eye).reshape(nseq, n_groups * 2 * n_new, n_groups * HEAD_DIM)


def kernel(x_prompt, x_sample, cache_a_k, cache_a_v, cache_idx_k, cache_b_k, cache_b_v, page_table,
           norm_gain, w_in, q_norm_a, k_norm_a, q_norm_b, k_norm_b, w_out):
    nb, seq, _ = x_prompt.shape
    nseq, n_new, _ = x_sample.shape
    n_pages = page_table.shape[1]
    past = n_pages * PAGE
    n_pool = cache_a_k.shape[1]
    assert norm_gain.shape[0] == 1 and seq % TK == 0 and 16 * n_new == 128 and nseq * n_new == TQ
    assert n_pages % IDX_PAGES == 0 and n_pages % KV_PAGES == 0
    assert n_pages * PAGE // MOBA_BLOCK <= 128 and seq // MOBA_BLOCK <= 128

    w_main, w_small = _pack_w_in(jnp.swapaxes(w_in, 1, 2))
    w_o = w_out[0].astype(BF16)
    gains = (q_norm_a, k_norm_a, q_norm_b, k_norm_b)

    xp = x_prompt.reshape(nb * seq, D_MODEL)
    xs = x_sample.reshape(nseq * n_new, D_MODEL)
    ms = xs.shape[0]

    mp = nb * seq
    h = _rmsnorm(xp, xs, norm_gain, ms)
    tm = (mp + ms) // 8
    proj = _matmul(h, w_main, tm, 1024, "proj")
    small = _matmul(h, w_small, tm, N_SMALL, "proj_small")

    cos_p, sin_p = _rope_tables(jnp.arange(seq))
    cos_s, sin_s = _rope_tables(past + jnp.arange(ms) % n_new)
    (qa_p, ka32_p, ka_p, va32_p, va_p, iq_p, ik32_p, ik_p, iw_p,
     qb_p, kb32_p, kb_p, vb32_p, vb_p, kmean_p) = _post(proj, small, cos_p, sin_p, *gains, mp, 0)
    (qa_s, ka32_s, _, va32_s, _, iq_s, ik32_s, _, iw_s,
     qb_s, kb32_s, _, vb32_s, _, _) = _post(proj, small, cos_s, sin_s, *gains, ms, mp)

    bra_p = _dsa_prompt(iq_p, iw_p, ik_p, qa_p, ka_p, va_p, proj, nb, seq)
    nblk = seq // MOBA_BLOCK
    km_p = jnp.pad(kmean_p.reshape(nb, nblk, 1024), ((0, 0), (0, 128 - nblk), (0, 0)))
    brb_p = _moba_prompt(qb_p, kb_p, vb_p, km_p, proj, nb, seq)
    y_p = _merge(xp, bra_p, brb_p, w_o, proj, 1024, 512, 0)

    pt_flat = page_table.reshape(-1).astype(I32)
    c_idx = cache_idx_k.reshape(n_pool, PAGE, IDX_DIM)
    pool = lambda c: c.reshape(n_pool, PAGE * c.shape[-2], HEAD_DIM)

    iq_ht = jnp.transpose(iq_s.reshape(nseq, n_new, IDX_HEADS, IDX_DIM), (0, 2, 1, 3))
    iq_ht = iq_ht.reshape(nseq, IDX_HEADS * n_new, IDX_DIM)
    scores = _idx_sample(pt_flat, c_idx, ik32_s, iq_ht, iw_s, nseq, n_pages, n_new)
    bias_s = _thresh(scores, past, n_new, 64)
    za = _to_grt(proj[mp:, OFF_AZ:OFF_AZ + BRANCH_W], nseq, n_new, A_KV)
    bra_s = _dsa_sample(pt_flat, pool(cache_a_k), pool(cache_a_v), ka32_s, va32_s,
                        _block_diag_q(qa_s, nseq, n_new, A_KV), bias_s, za, nseq, n_pages, n_new)
    bra_s = _from_grt(bra_s, nseq, n_new, A_KV)

    zb = _to_grt(proj[mp:, OFF_BZ:OFF_BZ + BRANCH_W], nseq, n_new, B_KV)
    brb_s = _moba_sample(pt_flat, pool(cache_b_k), pool(cache_b_v), kb32_s, vb32_s,
                         _block_diag_q(qb_s, nseq, n_new, B_KV), zb, nseq, n_pages, n_new)
    brb_s = _from_grt(brb_s, nseq, n_new, B_KV)
    y_s = _merge(xs, bra_s, brb_s, w_o, proj, ms, 512, mp)

    kv = lambda u, b, t: u.reshape(1, b, t, 8, HEAD_DIM)
    ix = lambda u, b, t: u.reshape(1, b, t, IDX_DIM)
    return (y_p.reshape(nb, seq, D_MODEL), y_s.reshape(nseq, n_new, D_MODEL),
            kv(ka32_p, nb, seq), kv(va32_p, nb, seq), ix(ik32_p, nb, seq),
            kv(kb32_p, nb, seq), kv(vb32_p, nb, seq),
            kv(ka32_s, nseq, n_new), kv(va32_s, nseq, n_new), ix(ik32_s, nseq, n_new),
            kv(kb32_s, nseq, n_new), kv(vb32_s, nseq, n_new))
```
